```python
import math
import jax, jax.numpy as jnp
from jax import lax
import numpy as np

D_MODEL = 1024
BATCH = 8
SEQ = 2048
DEPTH = 4
DEC_BATCH = 128
DEC_SEQ = 4
PAST_LEN = 2048
PAGE_SIZE = 128

A_WIDTH = D_MODEL // 2
A_V_DIM = 128
A_HEADS = A_WIDTH // A_V_DIM
A_QK_DIM = A_V_DIM // 2
Q_BLOCK = 128
B_WIDTH = D_MODEL // 4
B_BLOCKS = 4
B_BLOCK = B_WIDTH // B_BLOCKS
CONV_W = 4
RG_C = 8.0
C_WIDTH = D_MODEL // 4
C_HEADS = 4
C_DV = C_WIDTH // C_HEADS
C_DK = C_DV // 2
C_RANK = 16
GLA_TAU = 16.0
GLA_CHUNK = 64
D_MIX = A_WIDTH + B_WIDTH + C_WIDTH
D_FF = ((8 * D_MODEL // 3) + 127) // 128 * 128
IN_WIDTHS = (A_HEADS * 2 * A_QK_DIM, A_HEADS * 2 * A_QK_DIM, A_HEADS * A_V_DIM,
             B_WIDTH, B_WIDTH,
             C_HEADS * C_DK, C_HEADS * C_DK, C_WIDTH, C_WIDTH, C_RANK)
D_IN = sum(IN_WIDTHS)
RMS_EPS = 1e-6

kernel_name = 'hybrid_diffattn_rglru_gla_macaron_step'


def rms_norm(x, gain):
    xf = x.astype(jnp.float32)
    y = xf * lax.rsqrt(jnp.mean(xf * xf, axis=-1, keepdims=True) + RMS_EPS)
    return (y * gain.astype(jnp.float32)).astype(x.dtype)


def swiglu(x, wg, wu, wd):
    return (jax.nn.silu(x @ wg) * (x @ wu)) @ wd


def alibi_slopes():
    return jnp.asarray([2.0 ** (-8.0 * (h + 1) / A_HEADS) for h in range(A_HEADS)], jnp.float32)


def _diff_attn_block(q, q_pos, k, v, k_pos, lam, slopes):
    s = jnp.einsum('bqhmd,bkhmd->bhmqk', q, k).astype(jnp.float32) * (A_QK_DIM ** -0.5)
    dist = (q_pos[:, None] - k_pos[None, :]).astype(jnp.float32)
    s = s - slopes[None, :, None, None, None] * dist
    s = jnp.where(dist >= 0, s, -jnp.inf)
    p = jax.nn.softmax(s, axis=-1)
    w = p[:, :, 0] - lam * p[:, :, 1]
    return jnp.einsum('bhqk,bkhd->bqhd', w.astype(v.dtype), v)


def diff_attention(q, q_pos, k, v, k_pos, lam):
    B, L = q.shape[0], q.shape[1]
    qb = math.gcd(L, Q_BLOCK)
    nb = L // qb
    slopes = alibi_slopes()
    q_blocks = q.reshape(B, nb, qb, A_HEADS, 2, A_QK_DIM).swapaxes(0, 1)
    pos_blocks = q_pos.reshape(nb, qb)
    o = lax.map(lambda a: _diff_attn_block(a[0], a[1], k, v, k_pos, lam, slopes),
                (q_blocks, pos_blocks))
    return o.swapaxes(0, 1).reshape(B, L, A_HEADS, A_V_DIM)


def causal_conv(x, conv0, w, bias):
    L = x.shape[1]
    xc = jnp.concatenate([conv0.astype(x.dtype), x], axis=1)
    y = bias
    for j in range(CONV_W):
        y = y + w[j] * xc[:, j:j + L]
    return y, xc[:, L:]


def linear_scan(a, b, h0):
    b = b.at[:, 0].add(a[:, 0] * h0)

    def combine(lhs, rhs):
        al, bl = lhs
        ar, br = rhs
        return al * ar, ar * bl + br

    _, h = lax.associative_scan(combine, (a, b), axis=1)
    return h


def rg_lru(x, h0, wa, ba, wx, bx, lam):
    Bn, L, _ = x.shape
    xb = x.reshape(Bn, L, B_BLOCKS, B_BLOCK)
    r = jax.nn.sigmoid((jnp.einsum('blnd,nde->blne', xb, wa).reshape(Bn, L, B_WIDTH) + ba).astype(jnp.float32))
    i = jax.nn.sigmoid((jnp.einsum('blnd,nde->blne', xb, wx).reshape(Bn, L, B_WIDTH) + bx).astype(jnp.float32))
    log_a = -RG_C * jax.nn.softplus(-lam.astype(jnp.float32)) * r
    a = jnp.exp(log_a)
    u = jnp.sqrt(-jnp.expm1(2.0 * log_a)) * (i * x.astype(jnp.float32))
    h = linear_scan(a, u, h0.astype(jnp.float32))
    return h, h[:, -1]


def gla_chunked(q, k, v, log_a, S0):
    Bn, L, H, _ = q.shape
    dv = v.shape[-1]
    C = math.gcd(L, GLA_CHUNK)
    n = L // C

    def to_chunks(t):
        return t.astype(jnp.float32).reshape(Bn, n, C, H, t.shape[-1]).transpose(1, 0, 3, 2, 4)

    qc, kc, vc, gc = to_chunks(q), to_chunks(k), to_chunks(v), to_chunks(log_a)
    causal = jnp.tril(jnp.ones((C, C), bool))

    def step(S, inp):
        qi, ki, vi, gi = inp
        b = jnp.cumsum(gi, axis=2)
        b_last = b[:, :, -1:, :]
        o_inter = jnp.einsum('bhcd,bhde->bhce', qi * jnp.exp(b), S)
        diff = b[:, :, :, None, :] - b[:, :, None, :, :]
        decay = jnp.exp(jnp.where(causal[:, :, None], diff, -jnp.inf))
        att = jnp.einsum('bhid,bhjd,bhijd->bhij', qi, ki, decay)
        o_intra = jnp.einsum('bhij,bhje->bhie', att, vi)
        S_new = (jnp.exp(b_last)[:, :, 0, :, None] * S
                 + jnp.einsum('bhcd,bhce->bhde', ki * jnp.exp(b_last - b), vi))
        return S_new, o_inter + o_intra

    S, o = lax.scan(step, S0.astype(jnp.float32), (qc, kc, vc, gc))
    o = o.transpose(1, 0, 3, 2, 4).reshape(Bn, L, H, dv)
    return o, S


def decoder_layer(x, lam_init, p, past_k, past_v, conv0, h0, S0):
    Bn, L, _ = x.shape
    P = 0 if past_k is None else past_k.shape[1]
    dt = x.dtype
    x = x + 0.5 * swiglu(rms_norm(x, p['ffn1_norm']), p['ffn1_w_gate'], p['ffn1_w_up'], p['ffn1_w_down'])
    h = rms_norm(x, p['mix_norm'])
    z = h @ p['w_in']
    offs = [int(o) for o in np.cumsum(IN_WIDTHS)[:-1]]
    zq, zk, zv, zbx, zbg, zcq, zck, zcv, zcg, zca = jnp.split(z, offs, axis=-1)

    q = rms_norm(zq.reshape(Bn, L, A_HEADS, 2, A_QK_DIM), p['a_q_norm'])
    k = rms_norm(zk.reshape(Bn, L, A_HEADS, 2, A_QK_DIM), p['a_k_norm'])
    v = zv.reshape(Bn, L, A_HEADS, A_V_DIM)
    if past_k is None:
        k_all, v_all = k, v
    else:
        k_all = jnp.concatenate([past_k.astype(dt), k], axis=1)
        v_all = jnp.concatenate([past_v.astype(dt), v], axis=1)
    lv = p['a_lambda'].astype(jnp.float32)
    lam = jnp.exp(jnp.sum(lv[0] * lv[1])) - jnp.exp(jnp.sum(lv[2] * lv[3])) + lam_init
    o_a = diff_attention(q, P + jnp.arange(L), k_all, v_all, jnp.arange(P + L), lam)
    o_a = (rms_norm(o_a, p['a_out_norm']) * (1.0 - lam_init)).reshape(Bn, L, A_WIDTH).astype(dt)

    xc, conv_new = causal_conv(zbx, conv0, p['b_conv_w'], p['b_conv_b'])
    hb, h_last = rg_lru(xc, h0, p['b_gate_a_w'], p['b_gate_a_b'], p['b_gate_x_w'], p['b_gate_x_b'], p['b_lambda'])
    o_b = (hb * jax.nn.gelu(zbg.astype(jnp.float32))).astype(dt)

    cq = zcq.reshape(Bn, L, C_HEADS, C_DK) * (C_DK ** -0.5)
    ck = zck.reshape(Bn, L, C_HEADS, C_DK)
    cv = zcv.reshape(Bn, L, C_HEADS, C_DV)
    log_alpha = jax.nn.log_sigmoid((zca @ p['c_alpha_w2'] + p['c_alpha_b']).astype(jnp.float32)) / GLA_TAU
    o_c, S_new = gla_chunked(cq, ck, cv, log_alpha.reshape(Bn, L, C_HEADS, C_DK), S0)
    o_c = rms_norm(o_c.astype(dt), p['c_out_norm']) * jax.nn.silu(zcg.reshape(Bn, L, C_HEADS, C_DV))
    o_c = o_c.reshape(Bn, L, C_WIDTH).astype(dt)

    x = x + jnp.concatenate([o_a, o_b, o_c], axis=-1) @ p['w_out']
    x = x + 0.5 * swiglu(rms_norm(x, p['ffn2_norm']), p['ffn2_w_gate'], p['ffn2_w_up'], p['ffn2_w_down'])
    return x, (k, v, conv_new.astype(dt), h_last.astype(dt), S_new.astype(dt))


def setup_inputs(seed: int = 0) -> dict:
    key = jax.random.key(seed)
    ks = iter(jax.random.split(key, 48))
    f32 = jnp.float32
    n_pages = PAST_LEN // PAGE_SIZE
    n_pool = (DEC_BATCH * n_pages * 5) // 4

    def nrm(shape, scale):
        return jax.random.normal(next(ks), shape, f32) * scale

    def gain(shape):
        return 1.0 + nrm(shape, 0.01)

    x_prompt = nrm((BATCH, SEQ, D_MODEL), 1.0)
    x_sample = nrm((DEC_BATCH, DEC_SEQ, D_MODEL), 1.0)
    cache_k = nrm((DEPTH, n_pool, PAGE_SIZE, A_HEADS, 2, A_QK_DIM), 1.0)
    cache_v = nrm((DEPTH, n_pool, PAGE_SIZE, A_HEADS, A_V_DIM), 1.0)
    state_conv = nrm((DEPTH, DEC_BATCH, CONV_W - 1, B_WIDTH), 1.0)
    state_rglru = nrm((DEPTH, DEC_BATCH, B_WIDTH), 0.5)
    state_gla = nrm((DEPTH, DEC_BATCH, C_HEADS, C_DK, C_DV), 0.1)
    perm = jax.random.permutation(next(ks), n_pool)[:DEC_BATCH * n_pages]
    page_table = perm.reshape(DEC_BATCH, n_pages).astype(jnp.int32)

    u = jax.random.uniform(next(ks), (DEPTH, B_WIDTH), f32, 0.9, 0.999)
    s = u ** (1.0 / RG_C)
    b_lambda = jnp.log(s) - jnp.log1p(-s)

    return {
        'x_prompt': x_prompt,
        'x_sample': x_sample,
        'cache_k': cache_k,
        'cache_v': cache_v,
        'state_conv': state_conv,
        'state_rglru': state_rglru,
        'state_gla': state_gla,
        'page_table': page_table,
        'ffn1_norm': gain((DEPTH, D_MODEL)),
        'ffn1_w_gate': nrm((DEPTH, D_MODEL, D_FF), D_MODEL ** -0.5),
        'ffn1_w_up': nrm((DEPTH, D_MODEL, D_FF), D_MODEL ** -0.5),
        'ffn1_w_down': nrm((DEPTH, D_FF, D_MODEL), D_FF ** -0.5),
        'mix_norm': gain((DEPTH, D_MODEL)),
        'w_in': nrm((DEPTH, D_MODEL, D_IN), D_MODEL ** -0.5),
        'w_out': nrm((DEPTH, D_MIX, D_MODEL), D_MIX ** -0.5),
        'a_q_norm': gain((DEPTH, A_QK_DIM)),
        'a_k_norm': gain((DEPTH, A_QK_DIM)),
        'a_lambda': nrm((DEPTH, 4, A_QK_DIM), 0.1),
        'a_out_norm': gain((DEPTH, A_V_DIM)),
        'b_conv_w': nrm((DEPTH, CONV_W, B_WIDTH), CONV_W ** -0.5),
        'b_conv_b': nrm((DEPTH, B_WIDTH), 0.01),
        'b_gate_a_w': nrm((DEPTH, B_BLOCKS, B_BLOCK, B_BLOCK), B_BLOCK ** -0.5),
        'b_gate_a_b': nrm((DEPTH, B_WIDTH), 0.01),
        'b_gate_x_w': nrm((DEPTH, B_BLOCKS, B_BLOCK, B_BLOCK), B_BLOCK ** -0.5),
        'b_gate_x_b': nrm((DEPTH, B_WIDTH), 0.01),
        'b_lambda': b_lambda,
        'c_alpha_w2': nrm((DEPTH, C_RANK, C_HEADS * C_DK), C_RANK ** -0.5),
        'c_alpha_b': nrm((DEPTH, C_HEADS * C_DK), 0.01),
        'c_out_norm': gain((DEPTH, C_DV)),
        'ffn2_norm': gain((DEPTH, D_MODEL)),
        'ffn2_w_gate': nrm((DEPTH, D_MODEL, D_FF), D_MODEL ** -0.5),
        'ffn2_w_up': nrm((DEPTH, D_MODEL, D_FF), D_MODEL ** -0.5),
        'ffn2_w_down': nrm((DEPTH, D_FF, D_MODEL), D_FF ** -0.5),
    }


def reference(x_prompt, x_sample, cache_k, cache_v, state_conv, state_rglru, state_gla, page_table,
              ffn1_norm, ffn1_w_gate, ffn1_w_up, ffn1_w_down, mix_norm, w_in, w_out,
              a_q_norm, a_k_norm, a_lambda, a_out_norm,
              b_conv_w, b_conv_b, b_gate_a_w, b_gate_a_b, b_gate_x_w, b_gate_x_b, b_lambda,
              c_alpha_w2, c_alpha_b, c_out_norm,
              ffn2_norm, ffn2_w_gate, ffn2_w_up, ffn2_w_down):
    dt = x_prompt.dtype
    bp = x_prompt.shape[0]
    bs = x_sample.shape[0]
    yp, ys = x_prompt, x_sample
    outs_p = [[], [], [], [], []]
    outs_s = [[], [], [], [], []]
    for l in range(DEPTH):
        lam_init = 0.8 - 0.6 * math.exp(-0.3 * l)
        p = {
            'ffn1_norm': ffn1_norm[l], 'ffn1_w_gate': ffn1_w_gate[l], 'ffn1_w_up': ffn1_w_up[l],
            'ffn1_w_down': ffn1_w_down[l], 'mix_norm': mix_norm[l], 'w_in': w_in[l], 'w_out': w_out[l],
            'a_q_norm': a_q_norm[l], 'a_k_norm': a_k_norm[l], 'a_lambda': a_lambda[l],
            'a_out_norm': a_out_norm[l], 'b_conv_w': b_conv_w[l], 'b_conv_b': b_conv_b[l],
            'b_gate_a_w': b_gate_a_w[l], 'b_gate_a_b': b_gate_a_b[l], 'b_gate_x_w': b_gate_x_w[l],
            'b_gate_x_b': b_gate_x_b[l], 'b_lambda': b_lambda[l], 'c_alpha_w2': c_alpha_w2[l],
            'c_alpha_b': c_alpha_b[l], 'c_out_norm': c_out_norm[l], 'ffn2_norm': ffn2_norm[l],
            'ffn2_w_gate': ffn2_w_gate[l], 'ffn2_w_up': ffn2_w_up[l], 'ffn2_w_down': ffn2_w_down[l],
        }
        yp, st_p = decoder_layer(yp, lam_init, p, None, None,
                                 jnp.zeros((bp, CONV_W - 1, B_WIDTH), dt),
                                 jnp.zeros((bp, B_WIDTH), dt),
                                 jnp.zeros((bp, C_HEADS, C_DK, C_DV), jnp.float32))
        for lst, s in zip(outs_p, st_p):
            lst.append(s)
        past_k = cache_k[l, page_table].reshape(bs, -1, A_HEADS, 2, A_QK_DIM)
        past_v = cache_v[l, page_table].reshape(bs, -1, A_HEADS, A_V_DIM)
        ys, st_s = decoder_layer(ys, lam_init, p, past_k, past_v,
                                 state_conv[l], state_rglru[l], state_gla[l])
        for lst, s in zip(outs_s, st_s):
            lst.append(s)
    k_p, v_p, conv_p, h_p, S_p = [jnp.stack(o) for o in outs_p]
    k_s, v_s, conv_s, h_s, S_s = [jnp.stack(o) for o in outs_s]
    return (yp, ys, k_p, v_p, conv_p, h_p, S_p, k_s, v_s, conv_s, h_s, S_s)
```

```python
import functools
import math

import jax
import jax.numpy as jnp
from jax import lax
from jax.experimental import pallas as pl
from jax.experimental.pallas import tpu as pltpu

F32 = jnp.float32
BF16 = jnp.bfloat16

D_MODEL = 1024
DEPTH = 4
PAGE_SIZE = 128
A_HEADS = 4
A_V_DIM = 128
A_QK_DIM = 64
A_WIDTH = A_HEADS * A_V_DIM
B_WIDTH = 256
B_BLOCKS = 4
CONV_W = 4
RG_C = 8.0
C_WIDTH = 256
C_HEADS = 4
C_DV = 64
C_DK = 32
C_QK = C_HEADS * C_DK
C_RANK = 16
GLA_TAU = 16.0
D_FF = 2816
RMS_EPS = 1e-6

OFF_Q, OFF_K, OFF_V, OFF_BX, OFF_C = 0, 512, 1024, 1536, 2048
D_IN = 2832
D_IN_PAD = 2944

LANE = 128
SUBLANE = 8
VMEM_LIMIT_BYTES = 56 * 1024 * 1024

TOKEN_TILE = 512
FF_CHUNK = D_FF // 2
ATT_TILE = 256
SCAN_TILE = 128
GLA_BLOCK = 256
GLA_SUB = 16
SAMPLE_Q_ROWS = 16


def _cparams(*sem):
    return pltpu.CompilerParams(dimension_semantics=sem, vmem_limit_bytes=VMEM_LIMIT_BYTES)


def _const_spec(shape):
    zeros = (0,) * len(shape)
    return pl.BlockSpec(shape, lambda *_: zeros, pipeline_mode=pl.Buffered(1))


def _rms(x, gain):
    return x * lax.rsqrt(jnp.mean(x * x, axis=-1, keepdims=True) + RMS_EPS) * gain


def _softplus(y):
    return jnp.maximum(y, 0.0) + jnp.log1p(jnp.exp(-jnp.abs(y)))


def _gelu_tanh(x):
    return x * (0.5 * (1.0 + jnp.tanh(math.sqrt(2.0 / math.pi) * (x + 0.044715 * (x * x * x)))))


def _silu(x):
    return x * jax.nn.sigmoid(x)


def _group_matrix(n, group, value, dtype):
    shift = group.bit_length() - 1
    r = lax.shift_right_logical(lax.broadcasted_iota(jnp.int32, (n, n), 0), shift)
    c = lax.shift_right_logical(lax.broadcasted_iota(jnp.int32, (n, n), 1), shift)
    return jnp.where(r == c, value, 0.0).astype(dtype)


def _group_mean(sq, gmat):
    hi = sq.astype(BF16)
    lo = (sq - hi.astype(F32)).astype(BF16)
    return (jnp.dot(hi, gmat, preferred_element_type=F32)
            + jnp.dot(lo, gmat, preferred_element_type=F32))


def _ffn_math(x, gain, wg_ref, wu_ref, wd_ref):
    hb = _rms(x, gain).astype(BF16)
    acc = x
    for c in range(D_FF // FF_CHUNK):
        sl = slice(c * FF_CHUNK, (c + 1) * FF_CHUNK)
        a = jnp.dot(hb, wg_ref[:, sl], preferred_element_type=F32)
        u = jnp.dot(hb, wu_ref[:, sl], preferred_element_type=F32)
        t = (_silu(a) * u).astype(BF16)
        acc = acc + 0.5 * jnp.dot(t, wd_ref[sl, :], preferred_element_type=F32)
    return acc


def _ffn_body(x_ref, g_ref, wg_ref, wu_ref, wd_ref, o_ref):
    o_ref[...] = _ffn_math(x_ref[...], g_ref[...], wg_ref, wu_ref, wd_ref)


def _ffn_call(x, gain, wg, wu, wd):
    n = x.shape[0]
    tm = TOKEN_TILE
    row = lambda i: (i, 0)
    return pl.pallas_call(
        _ffn_body,
        grid=(n // tm,),
        in_specs=[pl.BlockSpec((tm, D_MODEL), row), _const_spec((1, D_MODEL)),
                  _const_spec((D_MODEL, D_FF)), _const_spec((D_MODEL, D_FF)),
                  _const_spec((D_FF, D_MODEL))],
        out_specs=pl.BlockSpec((tm, D_MODEL), row),
        out_shape=jax.ShapeDtypeStruct((n, D_MODEL), F32),
        compiler_params=_cparams("parallel"),
        name="ffn",
    )(x, gain, wg, wu, wd)


def _mixin_body(x_ref, g_ref, win_ref, gq_ref, gk_ref, w2_ref, ab_ref,
                q_ref, k_ref, kb_ref, v_ref, vb_ref, bx_ref, bg_ref,
                cq_ref, ck_ref, cv_ref, cg_ref, ga_ref):
    hb = _rms(x_ref[...], g_ref[...]).astype(BF16)
    gmat = _group_matrix(LANE, A_QK_DIM, 1.0 / A_QK_DIM, BF16)

    def qk_norm(z, gain):
        return z * lax.rsqrt(_group_mean(z * z, gmat) + RMS_EPS) * gain

    for c in range(A_WIDTH // LANE):
        sl = slice(c * LANE, (c + 1) * LANE)
        zq = jnp.dot(hb, win_ref[:, OFF_Q + c * LANE:OFF_Q + (c + 1) * LANE], preferred_element_type=F32)
        q_ref[:, sl] = (qk_norm(zq, gq_ref[...]) * (A_QK_DIM ** -0.5)).astype(BF16)
        zk = jnp.dot(hb, win_ref[:, OFF_K + c * LANE:OFF_K + (c + 1) * LANE], preferred_element_type=F32)
        kn = qk_norm(zk, gk_ref[...])
        k_ref[:, sl] = kn
        kb_ref[:, sl] = kn.astype(BF16)
    zv = jnp.dot(hb, win_ref[:, OFF_V:OFF_V + A_WIDTH], preferred_element_type=F32)
    v_ref[...] = zv
    vb_ref[...] = zv.astype(BF16)
    zb = jnp.dot(hb, win_ref[:, OFF_BX:OFF_BX + 2 * B_WIDTH], preferred_element_type=F32)
    bx_ref[...] = zb[:, :B_WIDTH]
    bg_ref[...] = _gelu_tanh(zb[:, B_WIDTH:])
    zc = jnp.dot(hb, win_ref[:, OFF_C:D_IN_PAD], preferred_element_type=F32)
    cq_ref[...] = zc[:, 0:C_QK] * (C_DK ** -0.5)
    ck_ref[...] = zc[:, C_QK:2 * C_QK]
    cv_ref[...] = zc[:, 2 * C_QK:2 * C_QK + C_WIDTH]
    cg_ref[...] = _silu(zc[:, 2 * C_QK + C_WIDTH:2 * C_QK + 2 * C_WIDTH])
    za = zc[:, 2 * C_QK + 2 * C_WIDTH:].astype(BF16)
    pre = jnp.dot(za, w2_ref[...], preferred_element_type=F32) + ab_ref[...]
    ga_ref[...] = -_softplus(-pre) * (1.0 / GLA_TAU)


def _mixin_call(x, gain, w_in, gq, gk, w2, ab):
    n = x.shape[0]
    tm = TOKEN_TILE
    row = lambda i: (i, 0)
    widths = [(A_WIDTH, BF16), (A_WIDTH, F32), (A_WIDTH, BF16), (A_WIDTH, F32), (A_WIDTH, BF16),
              (B_WIDTH, F32), (B_WIDTH, F32),
              (C_QK, F32), (C_QK, F32), (C_WIDTH, F32), (C_WIDTH, F32), (C_QK, F32)]
    return pl.pallas_call(
        _mixin_body,
        grid=(n // tm,),
        in_specs=[pl.BlockSpec((tm, D_MODEL), row), _const_spec((1, D_MODEL)),
                  _const_spec((D_MODEL, D_IN_PAD)), _const_spec((1, LANE)), _const_spec((1, LANE)),
                  _const_spec((LANE, C_QK)), _const_spec((1, C_QK))],
        out_specs=[pl.BlockSpec((tm, w), row) for w, _ in widths],
        out_shape=[jax.ShapeDtypeStruct((n, w), dt) for w, dt in widths],
        compiler_params=_cparams("parallel"),
        name="mixin",
    )(x, gain, w_in, gq, gk, w2, ab)


def _mixout_body(x_ref, oa_ref, ob_ref, oc_ref, wo_ref, g_ref, wg_ref, wu_ref, wd_ref, o_ref):
    mix = jnp.dot(oa_ref[...].astype(BF16), wo_ref[0:A_WIDTH, :], preferred_element_type=F32)
    mix = mix + jnp.dot(ob_ref[...].astype(BF16), wo_ref[A_WIDTH:A_WIDTH + B_WIDTH, :],
                        preferred_element_type=F32)
    mix = mix + jnp.dot(oc_ref[...].astype(BF16), wo_ref[A_WIDTH + B_WIDTH:, :],
                        preferred_element_type=F32)
    o_ref[...] = _ffn_math(x_ref[...] + mix, g_ref[...], wg_ref, wu_ref, wd_ref)


def _mixout_call(x, oa, ob, oc, w_out, gain, wg, wu, wd):
    n = x.shape[0]
    tm = TOKEN_TILE
    row = lambda i: (i, 0)
    return pl.pallas_call(
        _mixout_body,
        grid=(n // tm,),
        in_specs=[pl.BlockSpec((tm, D_MODEL), row), pl.BlockSpec((tm, A_WIDTH), row),
                  pl.BlockSpec((tm, B_WIDTH), row), pl.BlockSpec((tm, C_WIDTH), row),
                  _const_spec((D_MODEL, D_MODEL)), _const_spec((1, D_MODEL)),
                  _const_spec((D_MODEL, D_FF)), _const_spec((D_MODEL, D_FF)),
                  _const_spec((D_FF, D_MODEL))],
        out_specs=pl.BlockSpec((tm, D_MODEL), row),
        out_shape=jax.ShapeDtypeStruct((n, D_MODEL), F32),
        compiler_params=_cparams("parallel"),
        name="mixout_ffn",
    )(x, oa, ob, oc, w_out, gain, wg, wu, wd)


def _alibi_slope(h):
    return 2.0 ** (-8.0 * (h + 1) / A_HEADS)


def _diff_lambda(lv_ref, lam_init):
    lv = lv_ref[...]
    s01 = jnp.sum(lv[0:1, :] * lv[1:2, :], axis=1, keepdims=True)
    s23 = jnp.sum(lv[2:3, :] * lv[3:4, :], axis=1, keepdims=True)
    return jnp.exp(s01) - jnp.exp(s23) + lam_init


def _split_maps(qh, rows_per_map):
    lane = lax.broadcasted_iota(jnp.int32, qh.shape, 1)
    zero = jnp.zeros_like(qh)
    q1 = jnp.where(lane < A_QK_DIM, qh, zero)
    q2 = jnp.where(lane >= A_QK_DIM, qh, zero)
    return jnp.concatenate([q1, q2], axis=0)


def _attn_prompt_body(q_ref, k_ref, v_ref, lv_ref, go_ref, o_ref, m_sc, l_sc, acc_sc, *, lam_init):
    tq = ATT_TILE
    qi = pl.program_id(1)
    q0 = qi * tq
    lam = _diff_lambda(lv_ref, lam_init)
    row = lax.broadcasted_iota(jnp.int32, (2 * tq, 1), 0)
    qpos = q0 + jnp.where(row >= tq, row - tq, row)
    kcol = lax.broadcasted_iota(jnp.int32, (1, tq), 1)

    for h in range(A_HEADS):
        hs = slice(h * LANE, (h + 1) * LANE)
        slope = _alibi_slope(h)
        qx = _split_maps(q_ref[:, hs], tq)
        m_sc[...] = jnp.full((2 * tq, 1), -jnp.inf, F32)
        l_sc[...] = jnp.zeros((2 * tq, 1), F32)
        acc_sc[...] = jnp.zeros((2 * tq, LANE), F32)

        def tile(kj, masked):
            k0 = pl.multiple_of(kj * tq, tq)
            kt = k_ref[pl.ds(k0, tq), hs]
            vt = v_ref[pl.ds(k0, tq), hs]
            s = lax.dot_general(qx, kt, (((1,), (1,)), ((), ())), preferred_element_type=F32)
            kpos = k0 + kcol
            s = s + slope * (kpos - q0).astype(F32)
            if masked:
                s = jnp.where(kpos <= qpos, s, -jnp.inf)
            m_old = m_sc[...]
            m_new = jnp.maximum(m_old, jnp.max(s, axis=1, keepdims=True))
            alpha = jnp.exp(m_old - m_new)
            p = jnp.exp(s - m_new)
            l_sc[...] = alpha * l_sc[...] + jnp.sum(p, axis=1, keepdims=True)
            acc_sc[...] = alpha * acc_sc[...] + jnp.dot(p.astype(BF16), vt, preferred_element_type=F32)
            m_sc[...] = m_new

        def full_tile(kj, carry):
            tile(kj, False)
            return carry

        lax.fori_loop(0, qi, full_tile, 0)
        tile(qi, True)

        o = acc_sc[...] / l_sc[...]
        od = o[:tq] - lam * o[tq:]
        o_ref[:, hs] = _rms(od, go_ref[...]) * (1.0 - lam_init)


def _attn_prompt_call(qb, kb, vb, lv, go, lam_init, batch, seq):
    tq = ATT_TILE
    nq = seq // tq
    return pl.pallas_call(
        functools.partial(_attn_prompt_body, lam_init=lam_init),
        grid=(batch, nq),
        in_specs=[pl.BlockSpec((tq, A_WIDTH), lambda b, i: (b * nq + i, 0)),
                  pl.BlockSpec((seq, A_WIDTH), lambda b, i: (b, 0)),
                  pl.BlockSpec((seq, A_WIDTH), lambda b, i: (b, 0)),
                  _const_spec((4, A_QK_DIM)), _const_spec((1, A_V_DIM))],
        out_specs=pl.BlockSpec((tq, A_WIDTH), lambda b, i: (b * nq + i, 0)),
        out_shape=jax.ShapeDtypeStruct((batch * seq, A_WIDTH), F32),
        scratch_shapes=[pltpu.VMEM((2 * tq, 1), F32), pltpu.VMEM((2 * tq, 1), F32),
                        pltpu.VMEM((2 * tq, LANE), F32)],
        compiler_params=_cparams("parallel", "arbitrary"),
        name="attn_prompt",
    )(qb, kb, vb, lv, go)


def _attn_sample_body(pt_ref, q_ref, kn_ref, vn_ref, lv_ref, go_ref, *rest, lam_init, n_pages, n_new):
    del pt_ref
    k_pages = rest[:n_pages]
    v_pages = rest[n_pages:2 * n_pages]
    o_ref = rest[2 * n_pages]
    rows = SAMPLE_Q_ROWS
    past = n_pages * PAGE_SIZE
    lam = _diff_lambda(lv_ref, lam_init)
    row = lax.broadcasted_iota(jnp.int32, (rows, 1), 0)
    tok = jnp.bitwise_and(row, n_new - 1)
    qpos = past + tok
    lane = lax.broadcasted_iota(jnp.int32, (rows, LANE), 1)
    is_map1 = row < n_new
    is_map2 = jnp.logical_and(row >= n_new, row < 2 * n_new)
    kpos_past = lax.broadcasted_iota(jnp.int32, (1, past), 1)
    new_col = lax.broadcasted_iota(jnp.int32, (1, rows), 1)
    kpos_new = past + new_col

    for h in range(A_HEADS):
        hs = slice(h * LANE, (h + 1) * LANE)
        slope = _alibi_slope(h)
        qh = q_ref[:, hs].astype(F32)
        qx = jnp.where(lane < A_QK_DIM, jnp.where(is_map1, qh, 0.0), jnp.where(is_map2, qh, 0.0))
        qx = qx.astype(BF16)
        nt = (((1,), (1,)), ((), ()))
        s_parts = [lax.dot_general(qx, k_pages[p][:, hs].astype(BF16), nt, preferred_element_type=F32)
                   for p in range(n_pages)]
        s = jnp.concatenate(s_parts, axis=1)
        s = s - slope * (qpos - kpos_past).astype(F32)
        sn = lax.dot_general(qx, kn_ref[:, hs], nt, preferred_element_type=F32)
        sn = sn - slope * (qpos - kpos_new).astype(F32)
        sn = jnp.where(new_col <= tok, sn, -jnp.inf)
        m = jnp.maximum(jnp.max(s, axis=1, keepdims=True), jnp.max(sn, axis=1, keepdims=True))
        p = jnp.exp(s - m)
        pn = jnp.exp(sn - m)
        l = jnp.sum(p, axis=1, keepdims=True) + jnp.sum(pn, axis=1, keepdims=True)
        pb = p.astype(BF16)
        acc = jnp.dot(pn.astype(BF16), vn_ref[:, hs], preferred_element_type=F32)
        for pg in range(n_pages):
            acc = acc + jnp.dot(pb[:, pg * PAGE_SIZE:(pg + 1) * PAGE_SIZE],
                                v_pages[pg][:, hs].astype(BF16), preferred_element_type=F32)
        o = acc / l
        od = o[0:n_new] - lam * o[n_new:2 * n_new]
        o_ref[:, hs] = _rms(od, go_ref[...]) * (1.0 - lam_init)


def _attn_sample_call(page_table, qx, kn, vn, lv, go, cache_k, cache_v, layer, lam_init):
    batch, n_pages = page_table.shape
    n_new = 4
    rows = SAMPLE_Q_ROWS
    tok_spec = pl.BlockSpec((None, rows, A_WIDTH), lambda b, pt: (b, 0, 0))

    def page_spec(p):
        return pl.BlockSpec((None, None, PAGE_SIZE, A_WIDTH), lambda b, pt: (layer, pt[b, p], 0, 0))

    const = lambda shape: pl.BlockSpec(shape, lambda b, pt: (0, 0))
    grid_spec = pltpu.PrefetchScalarGridSpec(
        num_scalar_prefetch=1,
        grid=(batch,),
        in_specs=([tok_spec, tok_spec, tok_spec, const((4, A_QK_DIM)), const((1, A_V_DIM))]
                  + [page_spec(p) for p in range(n_pages)]
                  + [page_spec(p) for p in range(n_pages)]),
        out_specs=pl.BlockSpec((None, n_new, A_WIDTH), lambda b, pt: (b, 0, 0)),
    )
    return pl.pallas_call(
        functools.partial(_attn_sample_body, lam_init=lam_init, n_pages=n_pages, n_new=n_new),
        grid_spec=grid_spec,
        out_shape=jax.ShapeDtypeStruct((batch, n_new, A_WIDTH), F32),
        compiler_params=_cparams("parallel"),
        name="attn_sample",
    )(page_table, qx, kn, vn, lv, go, *([cache_k] * n_pages), *([cache_v] * n_pages))


def _expm1(x):
    u = jnp.exp(x)
    return jnp.where(u == 1.0, x, (u - 1.0) * x / jnp.log(u))


def _rglru_gates(xc, wa_ref, wx_ref, ba, bx, lam):
    xb = xc.astype(BF16)
    r = jax.nn.sigmoid(jnp.dot(xb, wa_ref[...], preferred_element_type=F32) + ba)
    i = jax.nn.sigmoid(jnp.dot(xb, wx_ref[...], preferred_element_type=F32) + bx)
    log_a = (-RG_C * _softplus(-lam)) * r
    a = jnp.exp(log_a)
    u = jnp.sqrt(-_expm1(2.0 * log_a)) * (i * xc)
    return a, u


def _scan_rows(a, b):
    n = a.shape[0]
    row = lax.broadcasted_iota(jnp.int32, a.shape, 0)
    s = 1
    while s < n:
        keep = row >= s
        b = jnp.where(keep, a * pltpu.roll(b, s, axis=0) + b, b)
        a = jnp.where(keep, a * pltpu.roll(a, s, axis=0), a)
        s *= 2
    return a, b


def _rglru_prompt_body(x_ref, gate_ref, c0_ref, h0_ref, cw_ref, cb_ref, wa_ref, wx_ref,
                       ba_ref, bx_ref, lam_ref, o_ref, tail_ref, hl_ref, *, seq):
    t = SCAN_TILE
    cw = cw_ref[...]
    cb, ba, bx, lam = cb_ref[...], ba_ref[...], bx_ref[...], lam_ref[...]

    def block(blk, h):
        t0 = pl.multiple_of(blk * t, t)
        xb = x_ref[pl.ds(t0, t), :]
        tp = pl.multiple_of(jnp.maximum(t0 - SUBLANE, 0), SUBLANE)
        prev = jnp.where(blk == 0, c0_ref[...], x_ref[pl.ds(tp, SUBLANE), :])
        xx = jnp.concatenate([prev, xb], axis=0)
        xc = cb
        for j in range(CONV_W):
            d = CONV_W - 1 - j
            sh = xb if d == 0 else pltpu.roll(xx, d, axis=0)[SUBLANE:SUBLANE + t]
            xc = xc + cw[j:j + 1, :] * sh
        a, u = _rglru_gates(xc, wa_ref, wx_ref, ba, bx, lam)
        pa, hb = _scan_rows(a, u)
        hs = hb + pa * h
        o_ref[pl.ds(t0, t), :] = hs * gate_ref[pl.ds(t0, t), :]
        return hs[t - 1:t, :]

    h_last = lax.fori_loop(0, seq // t, block, h0_ref[...])
    hl_ref[...] = h_last
    tail_ref[...] = x_ref[seq - SUBLANE:seq, :]


def _rglru_prompt_call(bx, gate, c0, h0, cw, cb, wa, wx, ba, bxb, lam, batch, seq):
    seq_spec = pl.BlockSpec((seq, B_WIDTH), lambda b: (b, 0))
    vec = _const_spec((1, B_WIDTH))
    return pl.pallas_call(
        functools.partial(_rglru_prompt_body, seq=seq),
        grid=(batch,),
        in_specs=[seq_spec, seq_spec,
                  pl.BlockSpec((None, SUBLANE, B_WIDTH), lambda b: (b, 0, 0)),
                  pl.BlockSpec((None, 1, B_WIDTH), lambda b: (b, 0, 0)),
                  _const_spec((CONV_W, B_WIDTH)), vec,
                  _const_spec((B_WIDTH, B_WIDTH)), _const_spec((B_WIDTH, B_WIDTH)), vec, vec, vec],
        out_specs=[seq_spec,
                   pl.BlockSpec((None, SUBLANE, B_WIDTH), lambda b: (b, 0, 0)),
                   pl.BlockSpec((None, 1, B_WIDTH), lambda b: (b, 0, 0))],
        out_shape=[jax.ShapeDtypeStruct((batch * seq, B_WIDTH), F32),
                   jax.ShapeDtypeStruct((batch, SUBLANE, B_WIDTH), F32),
                   jax.ShapeDtypeStruct((batch, 1, B_WIDTH), F32)],
        compiler_params=_cparams("parallel"),
        name="rglru_prompt",
    )(bx, gate, c0, h0, cw, cb, wa, wx, ba, bxb, lam)


def _rglru_sample_body(x_ref, gate_ref, c0_ref, h0_ref, cw_ref, cb_ref, wa_ref, wx_ref,
                       ba_ref, bx_ref, lam_ref, o_ref, cn_ref, hl_ref, *, steps):
    cw = cw_ref[...]
    cb, ba, bx, lam = cb_ref[...], ba_ref[...], bx_ref[...], lam_ref[...]
    xs = [c0_ref[j] for j in range(CONV_W - 1)] + [x_ref[s] for s in range(steps)]
    h = h0_ref[...]
    for s in range(steps):
        xc = cb
        for j in range(CONV_W):
            xc = xc + cw[j:j + 1, :] * xs[s + j]
        a, u = _rglru_gates(xc, wa_ref, wx_ref, ba, bx, lam)
        h = a * h + u
        o_ref[s] = h * gate_ref[s]
    for j in range(CONV_W - 1):
        cn_ref[j] = xs[steps + j]
    hl_ref[...] = h


def _rglru_sample_call(x_tm, gate_tm, c0_tm, h0, cw, cb, wa, wx, ba, bxb, lam):
    steps, batch, _ = x_tm.shape
    return pl.pallas_call(
        functools.partial(_rglru_sample_body, steps=steps),
        out_shape=[jax.ShapeDtypeStruct((steps, batch, B_WIDTH), F32),
                   jax.ShapeDtypeStruct((CONV_W - 1, batch, B_WIDTH), F32),
                   jax.ShapeDtypeStruct((batch, B_WIDTH), F32)],
        name="rglru_sample",
    )(x_tm, gate_tm, c0_tm, h0, cw, cb, wa, wx, ba, bxb, lam)


def _cumsum_rows(x):
    n = x.shape[0]
    row = lax.broadcasted_iota(jnp.int32, x.shape, 0)
    s = 1
    while s < n:
        x = x + jnp.where(row >= s, pltpu.roll(x, s, axis=0), 0.0)
        s *= 2
    return x


def _gla_body(q_ref, k_ref, v_ref, g_ref, gate_ref, gain_ref, s0_ref, o_ref, st_ref, s_sc, *, block, sub):
    tb = pl.program_id(1)

    @pl.when(tb == 0)
    def _():
        s_sc[...] = s0_ref[...]

    r = lax.shift_right_logical(lax.broadcasted_iota(jnp.int32, (C_QK, C_WIDTH), 0), 5)
    c = lax.shift_right_logical(lax.broadcasted_iota(jnp.int32, (C_QK, C_WIDTH), 1), 6)
    expand = jnp.where(r == c, 1.0, 0.0).astype(BF16)
    rt = lax.shift_right_logical(lax.broadcasted_iota(jnp.int32, (C_WIDTH, C_QK), 0), 6)
    ct = lax.shift_right_logical(lax.broadcasted_iota(jnp.int32, (C_WIDTH, C_QK), 1), 5)
    diag = jnp.where(rt == ct, 1.0, 0.0)
    gmat = _group_matrix(C_WIDTH, C_DV, 1.0 / C_DV, BF16)
    ii = lax.broadcasted_iota(jnp.int32, (sub, sub, C_QK), 0)
    jj = lax.broadcasted_iota(jnp.int32, (sub, sub, C_QK), 1)

    for ci in range(block // sub):
        rows = slice(ci * sub, (ci + 1) * sub)
        q, k, v = q_ref[rows, :], k_ref[rows, :], v_ref[rows, :]
        b = _cumsum_rows(g_ref[rows, :])
        st = s_sc[...]
        o_inter = lax.dot_general((q * jnp.exp(b)).astype(BF16), st.astype(BF16),
                                  (((1,), (1,)), ((), ())), preferred_element_type=F32)
        diff = b[:, None, :] - b[None, :, :]
        decay = jnp.exp(jnp.where(jj <= ii, diff, -jnp.inf))
        prod = decay * q[:, None, :] * k[None, :, :]
        att = jnp.dot(prod.reshape(sub * sub, C_QK).astype(BF16), expand, preferred_element_type=F32)
        o_intra = jnp.sum(att.reshape(sub, sub, C_WIDTH) * v[None, :, :], axis=1)
        o = o_inter + o_intra
        b_last = b[sub - 1:sub, :]
        kd = (k * jnp.exp(b_last - b)).astype(BF16)
        upd = lax.dot_general(v.astype(BF16), kd, (((0,), (0,)), ((), ())), preferred_element_type=F32)
        s_sc[...] = jnp.exp(b_last) * st + upd * diag
        on = o * lax.rsqrt(_group_mean(o * o, gmat) + RMS_EPS) * gain_ref[...]
        o_ref[rows, :] = on * gate_ref[rows, :]

    @pl.when(tb == pl.num_programs(1) - 1)
    def _():
        st_ref[...] = s_sc[...]


def _gla_call(cq, ck, cv, ga, gate, gain, s0t, batch, seq, block, sub):
    nb = seq // block
    tok = lambda w: pl.BlockSpec((block, w), lambda b, i: (b * nb + i, 0))
    st_spec = pl.BlockSpec((None, C_WIDTH, C_QK), lambda b, i: (b, 0, 0))
    return pl.pallas_call(
        functools.partial(_gla_body, block=block, sub=sub),
        grid=(batch, nb),
        in_specs=[tok(C_QK), tok(C_QK), tok(C_WIDTH), tok(C_QK), tok(C_WIDTH),
                  _const_spec((1, C_WIDTH)), st_spec],
        out_specs=[tok(C_WIDTH), st_spec],
        out_shape=[jax.ShapeDtypeStruct((batch * seq, C_WIDTH), F32),
                   jax.ShapeDtypeStruct((batch, C_WIDTH, C_QK), F32)],
        scratch_shapes=[pltpu.VMEM((C_WIDTH, C_QK), F32)],
        compiler_params=_cparams("parallel", "arbitrary"),
        name="gla",
    )(cq, ck, cv, ga, gate, gain, s0t)


def _state_to_blockdiag_t(s):
    out = jnp.zeros((s.shape[0], C_WIDTH, C_QK), F32)
    for h in range(C_HEADS):
        out = out.at[:, h * C_DV:(h + 1) * C_DV, h * C_DK:(h + 1) * C_DK].set(
            jnp.swapaxes(s[:, h].astype(F32), -1, -2))
    return out


def _blockdiag_t_to_state(st):
    blocks = [jnp.swapaxes(st[:, h * C_DV:(h + 1) * C_DV, h * C_DK:(h + 1) * C_DK], -1, -2)
              for h in range(C_HEADS)]
    return jnp.stack(blocks, axis=1)


def _layer_weights(l, w):
    row = lambda v: v.reshape(1, -1).astype(F32)
    pad_cols = D_IN_PAD - D_IN
    blockdiag = lambda m: jax.scipy.linalg.block_diag(*[m[i] for i in range(B_BLOCKS)]).astype(BF16)
    return dict(
        ffn1=(row(w['ffn1_norm'][l]), w['ffn1_w_gate'][l].astype(BF16), w['ffn1_w_up'][l].astype(BF16),
              w['ffn1_w_down'][l].astype(BF16)),
        ffn2=(row(w['ffn2_norm'][l]), w['ffn2_w_gate'][l].astype(BF16), w['ffn2_w_up'][l].astype(BF16),
              w['ffn2_w_down'][l].astype(BF16)),
        mix_norm=row(w['mix_norm'][l]),
        w_in=jnp.pad(w['w_in'][l], ((0, 0), (0, pad_cols))).astype(BF16),
        w_out=w['w_out'][l].astype(BF16),
        gq=row(jnp.tile(w['a_q_norm'][l], LANE // A_QK_DIM)),
        gk=row(jnp.tile(w['a_k_norm'][l], LANE // A_QK_DIM)),
        lv=w['a_lambda'][l].astype(F32),
        go=row(w['a_out_norm'][l]),
        cw=w['b_conv_w'][l].astype(F32),
        cb=row(w['b_conv_b'][l]),
        wa=blockdiag(w['b_gate_a_w'][l]),
        wx=blockdiag(w['b_gate_x_w'][l]),
        ba=row(w['b_gate_a_b'][l]),
        bxb=row(w['b_gate_x_b'][l]),
        lam=row(w['b_lambda'][l]),
        w2=jnp.pad(w['c_alpha_w2'][l], ((0, LANE - C_RANK), (0, 0))).astype(BF16),
        ab=row(w['c_alpha_b'][l]),
        gc=row(jnp.tile(w['c_out_norm'][l], C_HEADS)),
    )


def _prompt_layer(x, p, lam_init, batch, seq):
    x1 = _ffn_call(x, *p['ffn1'])
    qb, k, kb, v, vb, bx, bg, cq, ck, cv, cg, ga = _mixin_call(
        x1, p['mix_norm'], p['w_in'], p['gq'], p['gk'], p['w2'], p['ab'])
    oa = _attn_prompt_call(qb, kb, vb, p['lv'], p['go'], lam_init, batch, seq)
    ob, tail, h_last = _rglru_prompt_call(
        bx, bg, jnp.zeros((batch, SUBLANE, B_WIDTH), F32), jnp.zeros((batch, 1, B_WIDTH), F32),
        p['cw'], p['cb'], p['wa'], p['wx'], p['ba'], p['bxb'], p['lam'], batch, seq)
    oc, st = _gla_call(cq, ck, cv, ga, cg, p['gc'], jnp.zeros((batch, C_WIDTH, C_QK), F32),
                       batch, seq, GLA_BLOCK, GLA_SUB)
    x2 = _mixout_call(x1, oa, ob, oc, p['w_out'], *p['ffn2'])
    state = (k.reshape(batch, seq, A_HEADS, 2, A_QK_DIM), v.reshape(batch, seq, A_HEADS, A_V_DIM),
             tail[:, SUBLANE - (CONV_W - 1):, :], h_last[:, 0, :], _blockdiag_t_to_state(st))
    return x2, state


def _sample_layer(x, p, lam_init, layer, batch, steps, page_table, cache_k, cache_v, conv0, h0, s0):
    x1 = _ffn_call(x, *p['ffn1'])
    qb, k, kb, v, vb, bx, bg, cq, ck, cv, cg, ga = _mixin_call(
        x1, p['mix_norm'], p['w_in'], p['gq'], p['gk'], p['w2'], p['ab'])

    seqd = lambda a: a.reshape(batch, steps, a.shape[-1])
    q3 = seqd(qb)
    pad_rows = lambda a: jnp.pad(a, ((0, 0), (0, SAMPLE_Q_ROWS - a.shape[1]), (0, 0)))
    qx = pad_rows(jnp.concatenate([q3, q3], axis=1))
    oa = _attn_sample_call(page_table, qx, pad_rows(seqd(kb)), pad_rows(seqd(vb)), p['lv'], p['go'],
                           cache_k, cache_v, layer, lam_init)
    oa = oa.reshape(batch * steps, A_WIDTH)

    tm = lambda a: jnp.swapaxes(seqd(a), 0, 1)
    ob_tm, cn_tm, h_last = _rglru_sample_call(
        tm(bx), tm(bg), jnp.swapaxes(conv0.astype(F32), 0, 1), h0.astype(F32),
        p['cw'], p['cb'], p['wa'], p['wx'], p['ba'], p['bxb'], p['lam'])
    ob = jnp.swapaxes(ob_tm, 0, 1).reshape(batch * steps, B_WIDTH)

    pad8 = lambda a: jnp.pad(seqd(a), ((0, 0), (0, SUBLANE - steps), (0, 0))).reshape(batch * SUBLANE, -1)
    oc_pad, st = _gla_call(pad8(cq), pad8(ck), pad8(cv), pad8(ga), pad8(cg), p['gc'],
                           _state_to_blockdiag_t(s0), batch, SUBLANE, SUBLANE, SUBLANE)
    oc = oc_pad.reshape(batch, SUBLANE, C_WIDTH)[:, :steps].reshape(batch * steps, C_WIDTH)

    x2 = _mixout_call(x1, oa, ob, oc, p['w_out'], *p['ffn2'])
    state = (k.reshape(batch, steps, A_HEADS, 2, A_QK_DIM), v.reshape(batch, steps, A_HEADS, A_V_DIM),
             jnp.swapaxes(cn_tm, 0, 1), h_last, _blockdiag_t_to_state(st))
    return x2, state


def kernel(x_prompt, x_sample, cache_k, cache_v, state_conv, state_rglru, state_gla, page_table,
           ffn1_norm, ffn1_w_gate, ffn1_w_up, ffn1_w_down, mix_norm, w_in, w_out,
           a_q_norm, a_k_norm, a_lambda, a_out_norm,
           b_conv_w, b_conv_b, b_gate_a_w, b_gate_a_b, b_gate_x_w, b_gate_x_b, b_lambda,
           c_alpha_w2, c_alpha_b, c_out_norm,
           ffn2_norm, ffn2_w_gate, ffn2_w_up, ffn2_w_down):
    weights = dict(
        ffn1_norm=ffn1_norm, ffn1_w_gate=ffn1_w_gate, ffn1_w_up=ffn1_w_up, ffn1_w_down=ffn1_w_down,
        mix_norm=mix_norm, w_in=w_in, w_out=w_out, a_q_norm=a_q_norm, a_k_norm=a_k_norm,
        a_lambda=a_lambda, a_out_norm=a_out_norm, b_conv_w=b_conv_w, b_conv_b=b_conv_b,
        b_gate_a_w=b_gate_a_w, b_gate_a_b=b_gate_a_b, b_gate_x_w=b_gate_x_w, b_gate_x_b=b_gate_x_b,
        b_lambda=b_lambda, c_alpha_w2=c_alpha_w2, c_alpha_b=c_alpha_b, c_out_norm=c_out_norm,
        ffn2_norm=ffn2_norm, ffn2_w_gate=ffn2_w_gate, ffn2_w_up=ffn2_w_up, ffn2_w_down=ffn2_w_down)
    bp, seq, _ = x_prompt.shape
    bs, steps, _ = x_sample.shape
    depth = ffn1_norm.shape[0]
    n_pool = cache_k.shape[1]
    ck4 = cache_k.reshape(depth, n_pool, PAGE_SIZE, A_WIDTH)
    cv4 = cache_v.reshape(depth, n_pool, PAGE_SIZE, A_WIDTH)

    yp = x_prompt.reshape(bp * seq, D_MODEL)
    ys = x_sample.reshape(bs * steps, D_MODEL)
    outs_p, outs_s = [], []
    for l in range(depth):
        lam_init = 0.8 - 0.6 * math.exp(-0.3 * l)
        p = _layer_weights(l, weights)
        yp, st_p = _prompt_layer(yp, p, lam_init, bp, seq)
        ys, st_s = _sample_layer(ys, p, lam_init, l, bs, steps, page_table, ck4, cv4,
                                 state_conv[l], state_rglru[l], state_gla[l])
        outs_p.append(st_p)
        outs_s.append(st_s)
    stack = lambda outs: [jnp.stack([o[i] for o in outs]) for i in range(5)]
    k_p, v_p, conv_p, h_p, s_p = stack(outs_p)
    k_s, v_s, conv_s, h_s, s_s = stack(outs_s)
    return (yp.reshape(bp, seq, D_MODEL), ys.reshape(bs, steps, D_MODEL),
            k_p, v_p, conv_p, h_p, s_p, k_s, v_s, conv_s, h_s, s_s)
```

```python
import functools
import math

import jax
import jax.numpy as jnp
from jax import lax
from jax.experimental import pallas as pl
from jax.experimental.pallas import tpu as pltpu

F32 = jnp.float32
BF16 = jnp.bfloat16

D_MODEL = 1024
DEPTH = 4
PAGE_SIZE = 128
A_HEADS = 4
A_V_DIM = 128
A_QK_DIM = 64
A_WIDTH = A_HEADS * A_V_DIM
B_WIDTH = 256
B_BLOCKS = 4
CONV_W = 4
RG_C = 8.0
C_WIDTH = 256
C_HEADS = 4
C_DV = 64
C_DK = 32
C_QK = C_HEADS * C_DK
C_RANK = 16
GLA_TAU = 16.0
D_FF = 2816
RMS_EPS = 1e-6

OFF_Q, OFF_K, OFF_V, OFF_BX, OFF_C = 0, 512, 1024, 1536, 2048
D_IN = 2832
D_IN_PAD = 2944

LANE = 128
SUBLANE = 8
MXU_WIDTH = 256
VMEM_LIMIT_BYTES = 56 * 1024 * 1024

TOKEN_TILE = 512
FF_CHUNK = D_FF // 2
ATT_TILE = 256
SCAN_TILE = 128
GLA_BLOCK = 256
GLA_SUB = 16
GLA_SAMPLE_SEQS = 8
SAMPLE_Q_ROWS = 16


def _cparams(*sem):
    return pltpu.CompilerParams(dimension_semantics=sem, vmem_limit_bytes=VMEM_LIMIT_BYTES)


def _const_spec(shape):
    zeros = (0,) * len(shape)
    return pl.BlockSpec(shape, lambda *_: zeros, pipeline_mode=pl.Buffered(1))


def _rms(x, gain):
    return x * lax.rsqrt(jnp.mean(x * x, axis=-1, keepdims=True) + RMS_EPS) * gain


def _softplus(y):
    return jnp.maximum(y, 0.0) + jnp.log1p(jnp.exp(-jnp.abs(y)))


def _gelu_tanh(x):
    return x * (0.5 * (1.0 + jnp.tanh(math.sqrt(2.0 / math.pi) * (x + 0.044715 * (x * x * x)))))


def _silu(x):
    return x * jax.nn.sigmoid(x)


def _group_matrix(n, group, value, dtype):
    shift = group.bit_length() - 1
    r = lax.shift_right_logical(lax.broadcasted_iota(jnp.int32, (n, n), 0), shift)
    c = lax.shift_right_logical(lax.broadcasted_iota(jnp.int32, (n, n), 1), shift)
    return jnp.where(r == c, value, 0.0).astype(dtype)


def _group_mean(sq, gmat):
    return jnp.dot(sq.astype(BF16), gmat, preferred_element_type=F32)


def _ffn_math(x, gain, wg_ref, wu_ref, wd_ref):
    hb = _rms(x, gain).astype(BF16)
    acc = x
    for c in range(D_FF // FF_CHUNK):
        sl = slice(c * FF_CHUNK, (c + 1) * FF_CHUNK)
        a = jnp.dot(hb, wg_ref[:, sl], preferred_element_type=F32)
        u = jnp.dot(hb, wu_ref[:, sl], preferred_element_type=F32)
        t = (_silu(a) * u).astype(BF16)
        acc = acc + 0.5 * jnp.dot(t, wd_ref[sl, :], preferred_element_type=F32)
    return acc


def _ffn_body(x_ref, g_ref, wg_ref, wu_ref, wd_ref, o_ref):
    o_ref[...] = _ffn_math(x_ref[...], g_ref[...], wg_ref, wu_ref, wd_ref)


def _ffn_call(x, gain, wg, wu, wd):
    n = x.shape[0]
    tm = TOKEN_TILE
    row = lambda i: (i, 0)
    return pl.pallas_call(
        _ffn_body,
        grid=(n // tm,),
        in_specs=[pl.BlockSpec((tm, D_MODEL), row), _const_spec((1, D_MODEL)),
                  _const_spec((D_MODEL, D_FF)), _const_spec((D_MODEL, D_FF)),
                  _const_spec((D_FF, D_MODEL))],
        out_specs=pl.BlockSpec((tm, D_MODEL), row),
        out_shape=jax.ShapeDtypeStruct((n, D_MODEL), F32),
        compiler_params=_cparams("parallel"),
        name="ffn",
    )(x, gain, wg, wu, wd)


def _mixin_body(x_ref, g_ref, win_ref, gq_ref, gk_ref, w2_ref, ab_ref, k_all_ref, v_all_ref,
                q_ref, k_ref, kb_ref, v_ref, vb_ref, bx_ref, bg_ref,
                cq_ref, ck_ref, cv_ref, cg_ref, ga_ref):
    del k_all_ref, v_all_ref
    hb = _rms(x_ref[...], g_ref[...]).astype(BF16)
    gmat = _group_matrix(MXU_WIDTH, A_QK_DIM, 1.0 / A_QK_DIM, BF16)

    def qk_norm(z, gain):
        return z * lax.rsqrt(_group_mean(z * z, gmat) + RMS_EPS) * gain

    for c in range(A_WIDTH // MXU_WIDTH):
        sl = slice(c * MXU_WIDTH, (c + 1) * MXU_WIDTH)
        zq = jnp.dot(hb, win_ref[:, OFF_Q + c * MXU_WIDTH:OFF_Q + (c + 1) * MXU_WIDTH],
                     preferred_element_type=F32)
        q_ref[:, sl] = (qk_norm(zq, gq_ref[...]) * (A_QK_DIM ** -0.5)).astype(BF16)
        zk = jnp.dot(hb, win_ref[:, OFF_K + c * MXU_WIDTH:OFF_K + (c + 1) * MXU_WIDTH],
                     preferred_element_type=F32)
        kn = qk_norm(zk, gk_ref[...])
        k_ref[:, sl] = kn
        kb_ref[:, sl] = kn.astype(BF16)
    zv = jnp.dot(hb, win_ref[:, OFF_V:OFF_V + A_WIDTH], preferred_element_type=F32)
    v_ref[...] = zv
    vb_ref[...] = zv.astype(BF16)
    zb = jnp.dot(hb, win_ref[:, OFF_BX:OFF_BX + 2 * B_WIDTH], preferred_element_type=F32)
    bx_ref[...] = zb[:, :B_WIDTH]
    bg_ref[...] = _gelu_tanh(zb[:, B_WIDTH:])
    zc = jnp.dot(hb, win_ref[:, OFF_C:D_IN_PAD], preferred_element_type=F32)
    cq_ref[...] = zc[:, 0:C_QK] * (C_DK ** -0.5)
    ck_ref[...] = zc[:, C_QK:2 * C_QK]
    cv_ref[...] = zc[:, 2 * C_QK:2 * C_QK + C_WIDTH]
    cg_ref[...] = _silu(zc[:, 2 * C_QK + C_WIDTH:2 * C_QK + 2 * C_WIDTH])
    za = zc[:, 2 * C_QK + 2 * C_WIDTH:].astype(BF16)
    pre = jnp.dot(za, w2_ref[...], preferred_element_type=F32) + ab_ref[...]
    ga_ref[...] = -_softplus(-pre) * (1.0 / GLA_TAU)


def _mixin_call(x, gain, w_in, gq, gk, w2, ab, k_all, v_all, layer):
    n = x.shape[0]
    tm = TOKEN_TILE
    row = lambda i: (i, 0)
    widths = [(A_WIDTH, BF16), None, (A_WIDTH, BF16), None, (A_WIDTH, BF16),
              (B_WIDTH, F32), (B_WIDTH, F32),
              (C_QK, F32), (C_QK, F32), (C_WIDTH, F32), (C_WIDTH, F32), (C_QK, F32)]
    stacked_spec = pl.BlockSpec((None, tm, A_WIDTH), lambda i: (layer, i, 0))
    stacked_shape = jax.ShapeDtypeStruct(k_all.shape, F32)
    hbm = pl.BlockSpec(memory_space=pl.ANY)
    return pl.pallas_call(
        _mixin_body,
        grid=(n // tm,),
        in_specs=[pl.BlockSpec((tm, D_MODEL), row), _const_spec((1, D_MODEL)),
                  _const_spec((D_MODEL, D_IN_PAD)), _const_spec((1, MXU_WIDTH)),
                  _const_spec((1, MXU_WIDTH)), _const_spec((LANE, C_QK)), _const_spec((1, C_QK)),
                  hbm, hbm],
        out_specs=[stacked_spec if w is None else pl.BlockSpec((tm, w[0]), row) for w in widths],
        out_shape=[stacked_shape if w is None else jax.ShapeDtypeStruct((n, w[0]), w[1]) for w in widths],
        input_output_aliases={7: 1, 8: 3},
        compiler_params=_cparams("parallel"),
        name="mixin",
    )(x, gain, w_in, gq, gk, w2, ab, k_all, v_all)


def _mixout_body(x_ref, oa_ref, ob_ref, oc_ref, wo_ref, g_ref, wg_ref, wu_ref, wd_ref, o_ref):
    mix = jnp.dot(oa_ref[...].astype(BF16), wo_ref[0:A_WIDTH, :], preferred_element_type=F32)
    mix = mix + jnp.dot(ob_ref[...].astype(BF16), wo_ref[A_WIDTH:A_WIDTH + B_WIDTH, :],
                        preferred_element_type=F32)
    mix = mix + jnp.dot(oc_ref[...].astype(BF16), wo_ref[A_WIDTH + B_WIDTH:, :],
                        preferred_element_type=F32)
    o_ref[...] = _ffn_math(x_ref[...] + mix, g_ref[...], wg_ref, wu_ref, wd_ref)


def _mixout_call(x, oa, ob, oc, w_out, gain, wg, wu, wd):
    n = x.shape[0]
    tm = TOKEN_TILE
    row = lambda i: (i, 0)
    return pl.pallas_call(
        _mixout_body,
        grid=(n // tm,),
        in_specs=[pl.BlockSpec((tm, D_MODEL), row), pl.BlockSpec((tm, A_WIDTH), row),
                  pl.BlockSpec((tm, B_WIDTH), row), pl.BlockSpec((tm, C_WIDTH), row),
                  _const_spec((D_MODEL, D_MODEL)), _const_spec((1, D_MODEL)),
                  _const_spec((D_MODEL, D_FF)), _const_spec((D_MODEL, D_FF)),
                  _const_spec((D_FF, D_MODEL))],
        out_specs=pl.BlockSpec((tm, D_MODEL), row),
        out_shape=jax.ShapeDtypeStruct((n, D_MODEL), F32),
        compiler_params=_cparams("parallel"),
        name="mixout_ffn",
    )(x, oa, ob, oc, w_out, gain, wg, wu, wd)


def _alibi_slope(h):
    return 2.0 ** (-8.0 * (h + 1) / A_HEADS)


def _diff_lambda(lv_ref, lam_init):
    lv = lv_ref[...]
    s01 = jnp.sum(lv[0:1, :] * lv[1:2, :], axis=1, keepdims=True)
    s23 = jnp.sum(lv[2:3, :] * lv[3:4, :], axis=1, keepdims=True)
    return jnp.exp(s01) - jnp.exp(s23) + lam_init


def _attn_prompt_body(qt_ref, k_ref, vt_ref, lv_ref, go_ref, o_ref, qx_sc, bias_sc, m_sc, l_sc, acc_sc,
                      *, lam_init):
    t = ATT_TILE
    qi = pl.program_id(1)
    lam = _diff_lambda(lv_ref, lam_init)
    sub = lax.broadcasted_iota(jnp.int32, (LANE, t), 0)
    krow = lax.broadcasted_iota(jnp.int32, (t, 2 * t), 0)
    qcol = lax.broadcasted_iota(jnp.int32, (t, 2 * t), 1)
    causal = krow <= jnp.where(qcol >= t, qcol - t, qcol)
    kloc = krow.astype(F32)

    heads = [(h, slice(h * LANE, (h + 1) * LANE), _alibi_slope(h)) for h in range(A_HEADS)]

    for h, hs, slope in heads:
        qt = qt_ref[hs, :]
        zero = jnp.zeros_like(qt)
        qx_sc[h] = jnp.concatenate([jnp.where(sub < A_QK_DIM, qt, zero),
                                    jnp.where(sub >= A_QK_DIM, qt, zero)], axis=1)
        bias_sc[h] = slope * kloc
        m_sc[h] = jnp.full((1, 2 * t), -jnp.inf, F32)
        l_sc[h] = jnp.zeros((1, 2 * t), F32)
        acc_sc[h] = jnp.zeros((LANE, 2 * t), F32)

    def tile(kj, masked):
        k0 = pl.multiple_of(kj * t, t)
        dist = ((kj - qi) * t).astype(F32)
        for h, hs, slope in heads:
            kt = k_ref[pl.ds(k0, t), hs]
            vt = vt_ref[kj, hs, :]
            s = jnp.dot(kt, qx_sc[h], preferred_element_type=F32) + bias_sc[h]
            if masked:
                s = jnp.where(causal, s, -jnp.inf)
            c = slope * dist
            m_old = m_sc[h]
            m_new = jnp.maximum(m_old, jnp.max(s, axis=0, keepdims=True) + c)
            alpha = jnp.exp(m_old - m_new)
            p = jnp.exp(s - (m_new - c))
            l_sc[h] = alpha * l_sc[h] + jnp.sum(p, axis=0, keepdims=True)
            acc_sc[h] = alpha * acc_sc[h] + jnp.dot(vt, p.astype(BF16), preferred_element_type=F32)
            m_sc[h] = m_new

    def full_tile(kj, carry):
        tile(kj, False)
        return carry

    lax.fori_loop(0, qi, full_tile, 0)
    tile(qi, True)

    for h, hs, _ in heads:
        o = acc_sc[h] / l_sc[h]
        od = o[:, :t] - lam * o[:, t:]
        ms = jnp.mean(od * od, axis=0, keepdims=True)
        on = od * lax.rsqrt(ms + RMS_EPS) * go_ref[...] * (1.0 - lam_init)
        o_ref[:, hs] = on.T


def _attn_prompt_call(qb, kb, vb, lv, go_col, lam_init, batch, seq):
    t = ATT_TILE
    nt = seq // t
    tiles_t = lambda a: jnp.swapaxes(a.reshape(batch, nt, t, A_WIDTH), 2, 3)
    return pl.pallas_call(
        functools.partial(_attn_prompt_body, lam_init=lam_init),
        grid=(batch, nt),
        in_specs=[pl.BlockSpec((None, None, A_WIDTH, t), lambda b, i: (b, i, 0, 0)),
                  pl.BlockSpec((seq, A_WIDTH), lambda b, i: (b, 0)),
                  pl.BlockSpec((None, nt, A_WIDTH, t), lambda b, i: (b, 0, 0, 0)),
                  _const_spec((4, A_QK_DIM)), _const_spec((A_V_DIM, 1))],
        out_specs=pl.BlockSpec((t, A_WIDTH), lambda b, i: (b * nt + i, 0)),
        out_shape=jax.ShapeDtypeStruct((batch * seq, A_WIDTH), F32),
        scratch_shapes=[pltpu.VMEM((A_HEADS, LANE, 2 * t), BF16), pltpu.VMEM((A_HEADS, t, 2 * t), F32),
                        pltpu.VMEM((A_HEADS, 1, 2 * t), F32), pltpu.VMEM((A_HEADS, 1, 2 * t), F32),
                        pltpu.VMEM((A_HEADS, LANE, 2 * t), F32)],
        compiler_params=_cparams("parallel", "arbitrary"),
        name="attn_prompt",
    )(tiles_t(qb), kb, tiles_t(vb), lv, go_col)


def _attn_sample_body(pt_ref, q_ref, kn_ref, vn_ref, lv_ref, go_ref, *rest, lam_init, n_pages, n_new):
    del pt_ref
    k_pages = rest[:n_pages]
    v_pages = rest[n_pages:2 * n_pages]
    o_ref = rest[2 * n_pages]
    rows = SAMPLE_Q_ROWS
    past = n_pages * PAGE_SIZE
    lam = _diff_lambda(lv_ref, lam_init)
    row = lax.broadcasted_iota(jnp.int32, (rows, 1), 0)
    tok = jnp.bitwise_and(row, n_new - 1)
    qpos = past + tok
    lane = lax.broadcasted_iota(jnp.int32, (rows, LANE), 1)
    is_map1 = row < n_new
    is_map2 = jnp.logical_and(row >= n_new, row < 2 * n_new)
    kpos_past = lax.broadcasted_iota(jnp.int32, (1, past), 1)
    new_col = lax.broadcasted_iota(jnp.int32, (1, rows), 1)
    kpos_new = past + new_col

    for h in range(A_HEADS):
        hs = slice(h * LANE, (h + 1) * LANE)
        slope = _alibi_slope(h)
        qh = q_ref[:, hs].astype(F32)
        qx = jnp.where(lane < A_QK_DIM, jnp.where(is_map1, qh, 0.0), jnp.where(is_map2, qh, 0.0))
        qx = qx.astype(BF16)
        nt = (((1,), (1,)), ((), ()))
        s_parts = [jnp.dot(qx, k_pages[p][hs, :].astype(BF16), preferred_element_type=F32)
                   for p in range(n_pages)]
        s = jnp.concatenate(s_parts, axis=1)
        s = s - slope * (qpos - kpos_past).astype(F32)
        sn = lax.dot_general(qx, kn_ref[:, hs], nt, preferred_element_type=F32)
        sn = sn - slope * (qpos - kpos_new).astype(F32)
        sn = jnp.where(new_col <= tok, sn, -jnp.inf)
        m = jnp.maximum(jnp.max(s, axis=1, keepdims=True), jnp.max(sn, axis=1, keepdims=True))
        p = jnp.exp(s - m)
        pn = jnp.exp(sn - m)
        l = jnp.sum(p, axis=1, keepdims=True) + jnp.sum(pn, axis=1, keepdims=True)
        pb = p.astype(BF16)
        acc = jnp.dot(pn.astype(BF16), vn_ref[:, hs], preferred_element_type=F32)
        for pg in range(n_pages):
            acc = acc + jnp.dot(pb[:, pg * PAGE_SIZE:(pg + 1) * PAGE_SIZE],
                                v_pages[pg][pl.ds(h, PAGE_SIZE, stride=A_HEADS), :].astype(BF16),
                                preferred_element_type=F32)
        o = acc / l
        od = o[0:n_new] - lam * o[n_new:2 * n_new]
        o_ref[:, hs] = _rms(od, go_ref[...]) * (1.0 - lam_init)


def _attn_sample_call(page_table, qx, kn, vn, lv, go, cache_kt, cache_vr, layer, lam_init):
    batch, n_pages = page_table.shape
    n_new = 4
    rows = SAMPLE_Q_ROWS
    tok_spec = pl.BlockSpec((None, rows, A_WIDTH), lambda b, pt: (b, 0, 0))

    def k_spec(p):
        return pl.BlockSpec((None, None, A_WIDTH, PAGE_SIZE), lambda b, pt: (layer, pt[b, p], 0, 0))

    def v_spec(p):
        return pl.BlockSpec((None, None, PAGE_SIZE * A_HEADS, A_V_DIM),
                            lambda b, pt: (layer, pt[b, p], 0, 0))

    const = lambda shape: pl.BlockSpec(shape, lambda b, pt: (0, 0))
    grid_spec = pltpu.PrefetchScalarGridSpec(
        num_scalar_prefetch=1,
        grid=(batch,),
        in_specs=([tok_spec, tok_spec, tok_spec, const((4, A_QK_DIM)), const((1, A_V_DIM))]
                  + [k_spec(p) for p in range(n_pages)]
                  + [v_spec(p) for p in range(n_pages)]),
        out_specs=pl.BlockSpec((None, n_new, A_WIDTH), lambda b, pt: (b, 0, 0)),
    )
    return pl.pallas_call(
        functools.partial(_attn_sample_body, lam_init=lam_init, n_pages=n_pages, n_new=n_new),
        grid_spec=grid_spec,
        out_shape=jax.ShapeDtypeStruct((batch, n_new, A_WIDTH), F32),
        compiler_params=_cparams("parallel"),
        name="attn_sample",
    )(page_table, qx, kn, vn, lv, go, *([cache_kt] * n_pages), *([cache_vr] * n_pages))


def _expm1(x):
    u = jnp.exp(x)
    return jnp.where(u == 1.0, x, (u - 1.0) * x / jnp.log(u))


def _rglru_gates(xc, wa_ref, wx_ref, ba, bx, lam):
    xb = xc.astype(BF16)
    r = jax.nn.sigmoid(jnp.dot(xb, wa_ref[...], preferred_element_type=F32) + ba)
    i = jax.nn.sigmoid(jnp.dot(xb, wx_ref[...], preferred_element_type=F32) + bx)
    log_a = (-RG_C * _softplus(-lam)) * r
    a = jnp.exp(log_a)
    u = jnp.sqrt(-_expm1(2.0 * log_a)) * (i * xc)
    return a, u


def _scan_rows(a, b):
    n = a.shape[0]
    row = lax.broadcasted_iota(jnp.int32, a.shape, 0)
    s = 1
    while s < n:
        keep = row >= s
        b = jnp.where(keep, a * pltpu.roll(b, s, axis=0) + b, b)
        a = jnp.where(keep, a * pltpu.roll(a, s, axis=0), a)
        s *= 2
    return a, b


def _rglru_prompt_body(x_ref, gate_ref, c0_ref, h0_ref, cw_ref, cb_ref, wa_ref, wx_ref,
                       ba_ref, bx_ref, lam_ref, o_ref, tail_ref, hl_ref, *, seq):
    t = SCAN_TILE
    cw = cw_ref[...]
    cb, ba, bx, lam = cb_ref[...], ba_ref[...], bx_ref[...], lam_ref[...]

    def block(blk, h):
        t0 = pl.multiple_of(blk * t, t)
        xb = x_ref[pl.ds(t0, t), :]
        tp = pl.multiple_of(jnp.maximum(t0 - SUBLANE, 0), SUBLANE)
        prev = jnp.where(blk == 0, c0_ref[...], x_ref[pl.ds(tp, SUBLANE), :])
        xx = jnp.concatenate([prev, xb], axis=0)
        xc = cb
        for j in range(CONV_W):
            d = CONV_W - 1 - j
            sh = xb if d == 0 else pltpu.roll(xx, d, axis=0)[SUBLANE:SUBLANE + t]
            xc = xc + cw[j:j + 1, :] * sh
        a, u = _rglru_gates(xc, wa_ref, wx_ref, ba, bx, lam)
        pa, hb = _scan_rows(a, u)
        hs = hb + pa * h
        o_ref[pl.ds(t0, t), :] = hs * gate_ref[pl.ds(t0, t), :]
        return hs[t - 1:t, :]

    h_last = lax.fori_loop(0, seq // t, block, h0_ref[...])
    hl_ref[...] = h_last
    tail_ref[...] = x_ref[seq - SUBLANE:seq, :]


def _rglru_prompt_call(bx, gate, c0, h0, cw, cb, wa, wx, ba, bxb, lam, batch, seq):
    seq_spec = pl.BlockSpec((seq, B_WIDTH), lambda b: (b, 0))
    vec = _const_spec((1, B_WIDTH))
    return pl.pallas_call(
        functools.partial(_rglru_prompt_body, seq=seq),
        grid=(batch,),
        in_specs=[seq_spec, seq_spec,
                  pl.BlockSpec((None, SUBLANE, B_WIDTH), lambda b: (b, 0, 0)),
                  pl.BlockSpec((None, 1, B_WIDTH), lambda b: (b, 0, 0)),
                  _const_spec((CONV_W, B_WIDTH)), vec,
                  _const_spec((B_WIDTH, B_WIDTH)), _const_spec((B_WIDTH, B_WIDTH)), vec, vec, vec],
        out_specs=[seq_spec,
                   pl.BlockSpec((None, SUBLANE, B_WIDTH), lambda b: (b, 0, 0)),
                   pl.BlockSpec((None, 1, B_WIDTH), lambda b: (b, 0, 0))],
        out_shape=[jax.ShapeDtypeStruct((batch * seq, B_WIDTH), F32),
                   jax.ShapeDtypeStruct((batch, SUBLANE, B_WIDTH), F32),
                   jax.ShapeDtypeStruct((batch, 1, B_WIDTH), F32)],
        compiler_params=_cparams("parallel"),
        name="rglru_prompt",
    )(bx, gate, c0, h0, cw, cb, wa, wx, ba, bxb, lam)


def _rglru_sample_body(x_ref, gate_ref, c0_ref, h0_ref, cw_ref, cb_ref, wa_ref, wx_ref,
                       ba_ref, bx_ref, lam_ref, o_ref, cn_ref, hl_ref, *, steps):
    cw = cw_ref[...]
    cb, ba, bx, lam = cb_ref[...], ba_ref[...], bx_ref[...], lam_ref[...]
    xs = [c0_ref[j] for j in range(CONV_W - 1)] + [x_ref[s] for s in range(steps)]
    h = h0_ref[...]
    for s in range(steps):
        xc = cb
        for j in range(CONV_W):
            xc = xc + cw[j:j + 1, :] * xs[s + j]
        a, u = _rglru_gates(xc, wa_ref, wx_ref, ba, bx, lam)
        h = a * h + u
        o_ref[s] = h * gate_ref[s]
    for j in range(CONV_W - 1):
        cn_ref[j] = xs[steps + j]
    hl_ref[...] = h


def _rglru_sample_call(x_tm, gate_tm, c0_tm, h0, cw, cb, wa, wx, ba, bxb, lam):
    steps, batch, _ = x_tm.shape
    return pl.pallas_call(
        functools.partial(_rglru_sample_body, steps=steps),
        out_shape=[jax.ShapeDtypeStruct((steps, batch, B_WIDTH), F32),
                   jax.ShapeDtypeStruct((CONV_W - 1, batch, B_WIDTH), F32),
                   jax.ShapeDtypeStruct((batch, B_WIDTH), F32)],
        name="rglru_sample",
    )(x_tm, gate_tm, c0_tm, h0, cw, cb, wa, wx, ba, bxb, lam)


def _cumsum_rows(x):
    n = x.shape[0]
    row = lax.broadcasted_iota(jnp.int32, x.shape, 0)
    s = 1
    while s < n:
        x = x + jnp.where(row >= s, pltpu.roll(x, s, axis=0), 0.0)
        s *= 2
    return x


def _gla_body(q_ref, k_ref, v_ref, g_ref, gate_ref, gain_ref, s0_ref, o_ref, st_ref, s_sc,
              *, seqs, block, sub):
    tb = pl.program_id(1)

    @pl.when(tb == 0)
    def _():
        s_sc[...] = s0_ref[...]

    r = lax.shift_right_logical(lax.broadcasted_iota(jnp.int32, (C_QK, C_WIDTH), 0), 5)
    c = lax.shift_right_logical(lax.broadcasted_iota(jnp.int32, (C_QK, C_WIDTH), 1), 6)
    expand = jnp.where(r == c, 1.0, 0.0).astype(BF16)
    rt = lax.shift_right_logical(lax.broadcasted_iota(jnp.int32, (C_WIDTH, C_QK), 0), 6)
    ct = lax.shift_right_logical(lax.broadcasted_iota(jnp.int32, (C_WIDTH, C_QK), 1), 5)
    diag = jnp.where(rt == ct, 1.0, 0.0)
    gmat = _group_matrix(C_WIDTH, C_DV, 1.0 / C_DV, BF16)
    ii = lax.broadcasted_iota(jnp.int32, (sub, sub, C_QK), 0)
    jj = lax.broadcasted_iota(jnp.int32, (sub, sub, C_QK), 1)

    for sq, ci in [(sq, ci) for sq in range(seqs) for ci in range(block // sub)]:
        rows = slice(sq * block + ci * sub, sq * block + (ci + 1) * sub)
        q, k, v = q_ref[rows, :], k_ref[rows, :], v_ref[rows, :]
        b = _cumsum_rows(g_ref[rows, :])
        st = s_sc[sq]
        o_inter = lax.dot_general((q * jnp.exp(b)).astype(BF16), st.astype(BF16),
                                  (((1,), (1,)), ((), ())), preferred_element_type=F32)
        diff = b[:, None, :] - b[None, :, :]
        decay = jnp.exp(jnp.where(jj <= ii, diff, -jnp.inf))
        prod = decay * q[:, None, :] * k[None, :, :]
        att = jnp.dot(prod.reshape(sub * sub, C_QK).astype(BF16), expand, preferred_element_type=F32)
        o_intra = jnp.sum(att.reshape(sub, sub, C_WIDTH) * v[None, :, :], axis=1)
        o = o_inter + o_intra
        b_last = b[sub - 1:sub, :]
        kd = (k * jnp.exp(b_last - b)).astype(BF16)
        upd = lax.dot_general(v.astype(BF16), kd, (((0,), (0,)), ((), ())), preferred_element_type=F32)
        s_sc[sq] = jnp.exp(b_last) * st + upd * diag
        on = o * lax.rsqrt(_group_mean(o * o, gmat) + RMS_EPS) * gain_ref[...]
        o_ref[rows, :] = on * gate_ref[rows, :]

    @pl.when(tb == pl.num_programs(1) - 1)
    def _():
        st_ref[...] = s_sc[...]


def _gla_call(cq, ck, cv, ga, gate, gain, s0t, batch, seq, block, sub, seqs=1):
    nb = seq // block
    assert seqs == 1 or nb == 1
    tok = lambda w: pl.BlockSpec((seqs * block, w), lambda b, i: (b * nb + i, 0))
    st_spec = pl.BlockSpec((seqs, C_WIDTH, C_QK), lambda b, i: (b, 0, 0))
    return pl.pallas_call(
        functools.partial(_gla_body, seqs=seqs, block=block, sub=sub),
        grid=(batch // seqs, nb),
        in_specs=[tok(C_QK), tok(C_QK), tok(C_WIDTH), tok(C_QK), tok(C_WIDTH),
                  _const_spec((1, C_WIDTH)), st_spec],
        out_specs=[tok(C_WIDTH), st_spec],
        out_shape=[jax.ShapeDtypeStruct((batch * seq, C_WIDTH), F32),
                   jax.ShapeDtypeStruct((batch, C_WIDTH, C_QK), F32)],
        scratch_shapes=[pltpu.VMEM((seqs, C_WIDTH, C_QK), F32)],
        compiler_params=_cparams("parallel", "arbitrary"),
        name="gla",
    )(cq, ck, cv, ga, gate, gain, s0t)


def _state_to_blockdiag_t(s):
    out = jnp.zeros((s.shape[0], C_WIDTH, C_QK), F32)
    for h in range(C_HEADS):
        out = out.at[:, h * C_DV:(h + 1) * C_DV, h * C_DK:(h + 1) * C_DK].set(
            jnp.swapaxes(s[:, h].astype(F32), -1, -2))
    return out


def _blockdiag_t_to_state(st):
    blocks = [jnp.swapaxes(st[:, h * C_DV:(h + 1) * C_DV, h * C_DK:(h + 1) * C_DK], -1, -2)
              for h in range(C_HEADS)]
    return jnp.stack(blocks, axis=1)


def _layer_weights(l, w):
    row = lambda v: v.reshape(1, -1).astype(F32)
    pad_cols = D_IN_PAD - D_IN
    blockdiag = lambda m: jax.scipy.linalg.block_diag(*[m[i] for i in range(B_BLOCKS)]).astype(BF16)
    return dict(
        ffn1=(row(w['ffn1_norm'][l]), w['ffn1_w_gate'][l].astype(BF16), w['ffn1_w_up'][l].astype(BF16),
              w['ffn1_w_down'][l].astype(BF16)),
        ffn2=(row(w['ffn2_norm'][l]), w['ffn2_w_gate'][l].astype(BF16), w['ffn2_w_up'][l].astype(BF16),
              w['ffn2_w_down'][l].astype(BF16)),
        mix_norm=row(w['mix_norm'][l]),
        w_in=jnp.pad(w['w_in'][l], ((0, 0), (0, pad_cols))).astype(BF16),
        w_out=w['w_out'][l].astype(BF16),
        gq=row(jnp.tile(w['a_q_norm'][l], MXU_WIDTH // A_QK_DIM)),
        gk=row(jnp.tile(w['a_k_norm'][l], MXU_WIDTH // A_QK_DIM)),
        lv=w['a_lambda'][l].astype(F32),
        go=row(w['a_out_norm'][l]),
        go_col=w['a_out_norm'][l].reshape(-1, 1).astype(F32),
        cw=w['b_conv_w'][l].astype(F32),
        cb=row(w['b_conv_b'][l]),
        wa=blockdiag(w['b_gate_a_w'][l]),
        wx=blockdiag(w['b_gate_x_w'][l]),
        ba=row(w['b_gate_a_b'][l]),
        bxb=row(w['b_gate_x_b'][l]),
        lam=row(w['b_lambda'][l]),
        w2=jnp.pad(w['c_alpha_w2'][l], ((0, LANE - C_RANK), (0, 0))).astype(BF16),
        ab=row(w['c_alpha_b'][l]),
        gc=row(jnp.tile(w['c_out_norm'][l], C_HEADS)),
    )


def _prompt_layer(x, p, lam_init, layer, batch, seq, k_all, v_all):
    x1 = _ffn_call(x, *p['ffn1'])
    qb, k_all, kb, v_all, vb, bx, bg, cq, ck, cv, cg, ga = _mixin_call(
        x1, p['mix_norm'], p['w_in'], p['gq'], p['gk'], p['w2'], p['ab'], k_all, v_all, layer)
    oa = _attn_prompt_call(qb, kb, vb, p['lv'], p['go_col'], lam_init, batch, seq)
    ob, tail, h_last = _rglru_prompt_call(
        bx, bg, jnp.zeros((batch, SUBLANE, B_WIDTH), F32), jnp.zeros((batch, 1, B_WIDTH), F32),
        p['cw'], p['cb'], p['wa'], p['wx'], p['ba'], p['bxb'], p['lam'], batch, seq)
    oc, st = _gla_call(cq, ck, cv, ga, cg, p['gc'], jnp.zeros((batch, C_WIDTH, C_QK), F32),
                       batch, seq, GLA_BLOCK, GLA_SUB)
    x2 = _mixout_call(x1, oa, ob, oc, p['w_out'], *p['ffn2'])
    state = (tail[:, SUBLANE - (CONV_W - 1):, :], h_last[:, 0, :], _blockdiag_t_to_state(st))
    return x2, k_all, v_all, state


def _sample_layer(x, p, lam_init, layer, batch, steps, page_table, cache_k, cache_v, conv0, h0, s0,
                  k_all, v_all):
    x1 = _ffn_call(x, *p['ffn1'])
    qb, k_all, kb, v_all, vb, bx, bg, cq, ck, cv, cg, ga = _mixin_call(
        x1, p['mix_norm'], p['w_in'], p['gq'], p['gk'], p['w2'], p['ab'], k_all, v_all, layer)

    seqd = lambda a: a.reshape(batch, steps, a.shape[-1])
    q3 = seqd(qb)
    pad_rows = lambda a: jnp.pad(a, ((0, 0), (0, SAMPLE_Q_ROWS - a.shape[1]), (0, 0)))
    qx = pad_rows(jnp.concatenate([q3, q3], axis=1))
    oa = _attn_sample_call(page_table, qx, pad_rows(seqd(kb)), pad_rows(seqd(vb)), p['lv'], p['go'],
                           cache_k, cache_v, layer, lam_init)
    oa = oa.reshape(batch * steps, A_WIDTH)

    tm = lambda a: jnp.swapaxes(seqd(a), 0, 1)
    ob_tm, cn_tm, h_last = _rglru_sample_call(
        tm(bx), tm(bg), jnp.swapaxes(conv0.astype(F32), 0, 1), h0.astype(F32),
        p['cw'], p['cb'], p['wa'], p['wx'], p['ba'], p['bxb'], p['lam'])
    ob = jnp.swapaxes(ob_tm, 0, 1).reshape(batch * steps, B_WIDTH)

    pad8 = lambda a: jnp.pad(seqd(a), ((0, 0), (0, SUBLANE - steps), (0, 0))).reshape(batch * SUBLANE, -1)
    oc_pad, st = _gla_call(pad8(cq), pad8(ck), pad8(cv), pad8(ga), pad8(cg), p['gc'],
                           _state_to_blockdiag_t(s0), batch, SUBLANE, SUBLANE, SUBLANE,
                           seqs=GLA_SAMPLE_SEQS)
    oc = oc_pad.reshape(batch, SUBLANE, C_WIDTH)[:, :steps].reshape(batch * steps, C_WIDTH)

    x2 = _mixout_call(x1, oa, ob, oc, p['w_out'], *p['ffn2'])
    state = (jnp.swapaxes(cn_tm, 0, 1), h_last, _blockdiag_t_to_state(st))
    return x2, k_all, v_all, state


def kernel(x_prompt, x_sample, cache_k, cache_v, state_conv, state_rglru, state_gla, page_table,
           ffn1_norm, ffn1_w_gate, ffn1_w_up, ffn1_w_down, mix_norm, w_in, w_out,
           a_q_norm, a_k_norm, a_lambda, a_out_norm,
           b_conv_w, b_conv_b, b_gate_a_w, b_gate_a_b, b_gate_x_w, b_gate_x_b, b_lambda,
           c_alpha_w2, c_alpha_b, c_out_norm,
           ffn2_norm, ffn2_w_gate, ffn2_w_up, ffn2_w_down):
    weights = dict(
        ffn1_norm=ffn1_norm, ffn1_w_gate=ffn1_w_gate, ffn1_w_up=ffn1_w_up, ffn1_w_down=ffn1_w_down,
        mix_norm=mix_norm, w_in=w_in, w_out=w_out, a_q_norm=a_q_norm, a_k_norm=a_k_norm,
        a_lambda=a_lambda, a_out_norm=a_out_norm, b_conv_w=b_conv_w, b_conv_b=b_conv_b,
        b_gate_a_w=b_gate_a_w, b_gate_a_b=b_gate_a_b, b_gate_x_w=b_gate_x_w, b_gate_x_b=b_gate_x_b,
        b_lambda=b_lambda, c_alpha_w2=c_alpha_w2, c_alpha_b=c_alpha_b, c_out_norm=c_out_norm,
        ffn2_norm=ffn2_norm, ffn2_w_gate=ffn2_w_gate, ffn2_w_up=ffn2_w_up, ffn2_w_down=ffn2_w_down)
    bp, seq, _ = x_prompt.shape
    bs, steps, _ = x_sample.shape
    depth = ffn1_norm.shape[0]
    n_pool = cache_k.shape[1]
    ckt = jnp.transpose(cache_k, (0, 1, 3, 4, 5, 2)).reshape(depth, n_pool, A_WIDTH, PAGE_SIZE)
    cvh = cache_v.reshape(depth, n_pool, PAGE_SIZE * A_HEADS, A_V_DIM)

    yp = x_prompt.reshape(bp * seq, D_MODEL)
    ys = x_sample.reshape(bs * steps, D_MODEL)
    kp_all = jnp.zeros((depth, bp * seq, A_WIDTH), F32)
    vp_all = jnp.zeros((depth, bp * seq, A_WIDTH), F32)
    ks_all = jnp.zeros((depth, bs * steps, A_WIDTH), F32)
    vs_all = jnp.zeros((depth, bs * steps, A_WIDTH), F32)
    outs_p, outs_s = [], []
    for l in range(depth):
        lam_init = 0.8 - 0.6 * math.exp(-0.3 * l)
        p = _layer_weights(l, weights)
        yp, kp_all, vp_all, st_p = _prompt_layer(yp, p, lam_init, l, bp, seq, kp_all, vp_all)
        ys, ks_all, vs_all, st_s = _sample_layer(ys, p, lam_init, l, bs, steps, page_table, ckt, cvh,
                                                 state_conv[l], state_rglru[l], state_gla[l],
                                                 ks_all, vs_all)
        outs_p.append(st_p)
        outs_s.append(st_s)
    stack = lambda outs: [jnp.stack([o[i] for o in outs]) for i in range(3)]
    conv_p, h_p, s_p = stack(outs_p)
    conv_s, h_s, s_s = stack(outs_s)
    return (yp.reshape(bp, seq, D_MODEL), ys.reshape(bs, steps, D_MODEL),
            kp_all.reshape(depth, bp, seq, A_HEADS, 2, A_QK_DIM),
            vp_all.reshape(depth, bp, seq, A_HEADS, A_V_DIM), conv_p, h_p, s_p,
            ks_all.reshape(depth, bs, steps, A_HEADS, 2, A_QK_DIM),
            vs_all.reshape(depth, bs, steps, A_HEADS, A_V_DIM), conv_s, h_s, s_s)
```

```python
import functools
import math

import jax
import jax.numpy as jnp
from jax import lax
from jax.experimental import pallas as pl
from jax.experimental.pallas import tpu as pltpu

F32 = jnp.float32
BF16 = jnp.bfloat16

D_MODEL = 1024
DEPTH = 4
PAGE_SIZE = 128
A_HEADS = 4
A_V_DIM = 128
A_QK_DIM = 64
A_WIDTH = A_HEADS * A_V_DIM
B_WIDTH = 256
B_BLOCKS = 4
CONV_W = 4
RG_C = 8.0
C_WIDTH = 256
C_HEADS = 4
C_DV = 64
C_DK = 32
C_QK = C_HEADS * C_DK
C_RANK = 16
GLA_TAU = 16.0
D_FF = 2816
RMS_EPS = 1e-6

OFF_Q, OFF_K, OFF_V, OFF_BX, OFF_C = 0, 512, 1024, 1536, 2048
D_IN = 2832
D_IN_PAD = 2944

LANE = 128
SUBLANE = 8
MXU_WIDTH = 256
VMEM_LIMIT_BYTES = 56 * 1024 * 1024

TOKEN_TILE = 512
FF_CHUNK = D_FF // 2
ATT_TILE = 256
SCAN_TILE = 128
GLA_BLOCK = 256
GLA_SUB = 16
GLA_SAMPLE_SEQS = 8
SAMPLE_Q_ROWS = 16


def _cparams(*sem):
    return pltpu.CompilerParams(dimension_semantics=sem, vmem_limit_bytes=VMEM_LIMIT_BYTES)


def _const_spec(shape):
    zeros = (0,) * len(shape)
    return pl.BlockSpec(shape, lambda *_: zeros, pipeline_mode=pl.Buffered(1))


def _rms(x, gain):
    return x * lax.rsqrt(jnp.mean(x * x, axis=-1, keepdims=True) + RMS_EPS) * gain


def _softplus(y):
    return jnp.maximum(y, 0.0) + jnp.log1p(jnp.exp(-jnp.abs(y)))


def _gelu_tanh(x):
    return x * (0.5 * (1.0 + jnp.tanh(math.sqrt(2.0 / math.pi) * (x + 0.044715 * (x * x * x)))))


def _silu(x):
    return x * jax.nn.sigmoid(x)


def _group_matrix(n, group, value, dtype):
    shift = group.bit_length() - 1
    r = lax.shift_right_logical(lax.broadcasted_iota(jnp.int32, (n, n), 0), shift)
    c = lax.shift_right_logical(lax.broadcasted_iota(jnp.int32, (n, n), 1), shift)
    return jnp.where(r == c, value, 0.0).astype(dtype)


def _group_mean(sq, gmat):
    return jnp.dot(sq.astype(BF16), gmat, preferred_element_type=F32)


def _ffn_math(x, gain, wg_ref, wu_ref, wd_ref):
    hb = _rms(x, gain).astype(BF16)
    acc = x
    for c in range(D_FF // FF_CHUNK):
        sl = slice(c * FF_CHUNK, (c + 1) * FF_CHUNK)
        a = jnp.dot(hb, wg_ref[:, sl], preferred_element_type=F32)
        u = jnp.dot(hb, wu_ref[:, sl], preferred_element_type=F32)
        t = (_silu(a) * u).astype(BF16)
        acc = acc + 0.5 * jnp.dot(t, wd_ref[sl, :], preferred_element_type=F32)
    return acc


def _ffn_body(x_ref, g_ref, wg_ref, wu_ref, wd_ref, o_ref):
    o_ref[...] = _ffn_math(x_ref[...], g_ref[...], wg_ref, wu_ref, wd_ref)


def _ffn_call(x, gain, wg, wu, wd):
    n = x.shape[0]
    tm = TOKEN_TILE
    row = lambda i: (i, 0)
    return pl.pallas_call(
        _ffn_body,
        grid=(n // tm,),
        in_specs=[pl.BlockSpec((tm, D_MODEL), row), _const_spec((1, D_MODEL)),
                  _const_spec((D_MODEL, D_FF)), _const_spec((D_MODEL, D_FF)),
                  _const_spec((D_FF, D_MODEL))],
        out_specs=pl.BlockSpec((tm, D_MODEL), row),
        out_shape=jax.ShapeDtypeStruct((n, D_MODEL), F32),
        compiler_params=_cparams("parallel"),
        name="ffn",
    )(x, gain, wg, wu, wd)


def _mixin_body(x_ref, g_ref, win_ref, gq_ref, gk_ref, w2_ref, ab_ref,
                q_ref, k_ref, kb_ref, v_ref, vb_ref, bx_ref, bg_ref,
                cq_ref, ck_ref, cv_ref, cg_ref, ga_ref):
    hb = _rms(x_ref[...], g_ref[...]).astype(BF16)
    gmat = _group_matrix(MXU_WIDTH, A_QK_DIM, 1.0 / A_QK_DIM, BF16)

    def qk_norm(z, gain):
        return z * lax.rsqrt(_group_mean(z * z, gmat) + RMS_EPS) * gain

    for c in range(A_WIDTH // MXU_WIDTH):
        sl = slice(c * MXU_WIDTH, (c + 1) * MXU_WIDTH)
        zq = jnp.dot(hb, win_ref[:, OFF_Q + c * MXU_WIDTH:OFF_Q + (c + 1) * MXU_WIDTH],
                     preferred_element_type=F32)
        q_ref[:, sl] = (qk_norm(zq, gq_ref[...]) * (A_QK_DIM ** -0.5)).astype(BF16)
        zk = jnp.dot(hb, win_ref[:, OFF_K + c * MXU_WIDTH:OFF_K + (c + 1) * MXU_WIDTH],
                     preferred_element_type=F32)
        kn = qk_norm(zk, gk_ref[...])
        k_ref[:, sl] = kn
        kb_ref[:, sl] = kn.astype(BF16)
    zv = jnp.dot(hb, win_ref[:, OFF_V:OFF_V + A_WIDTH], preferred_element_type=F32)
    v_ref[...] = zv
    vb_ref[...] = zv.astype(BF16)
    zb = jnp.dot(hb, win_ref[:, OFF_BX:OFF_BX + 2 * B_WIDTH], preferred_element_type=F32)
    bx_ref[...] = zb[:, :B_WIDTH]
    bg_ref[...] = _gelu_tanh(zb[:, B_WIDTH:])
    zc = jnp.dot(hb, win_ref[:, OFF_C:D_IN_PAD], preferred_element_type=F32)
    cq_ref[...] = zc[:, 0:C_QK] * (C_DK ** -0.5)
    ck_ref[...] = zc[:, C_QK:2 * C_QK]
    cv_ref[...] = zc[:, 2 * C_QK:2 * C_QK + C_WIDTH]
    cg_ref[...] = _silu(zc[:, 2 * C_QK + C_WIDTH:2 * C_QK + 2 * C_WIDTH])
    za = zc[:, 2 * C_QK + 2 * C_WIDTH:].astype(BF16)
    pre = jnp.dot(za, w2_ref[...], preferred_element_type=F32) + ab_ref[...]
    ga_ref[...] = -_softplus(-pre) * (1.0 / GLA_TAU)


def _mixin_call(x, gain, w_in, gq, gk, w2, ab):
    n = x.shape[0]
    tm = TOKEN_TILE
    row = lambda i: (i, 0)
    widths = [(A_WIDTH, BF16), (A_WIDTH, F32), (A_WIDTH, BF16), (A_WIDTH, F32), (A_WIDTH, BF16),
              (B_WIDTH, F32), (B_WIDTH, F32),
              (C_QK, F32), (C_QK, F32), (C_WIDTH, F32), (C_WIDTH, F32), (C_QK, F32)]
    return pl.pallas_call(
        _mixin_body,
        grid=(n // tm,),
        in_specs=[pl.BlockSpec((tm, D_MODEL), row), _const_spec((1, D_MODEL)),
                  _const_spec((D_MODEL, D_IN_PAD)), _const_spec((1, MXU_WIDTH)),
                  _const_spec((1, MXU_WIDTH)), _const_spec((LANE, C_QK)), _const_spec((1, C_QK))],
        out_specs=[pl.BlockSpec((tm, w), row) for w, _ in widths],
        out_shape=[jax.ShapeDtypeStruct((n, w), dt) for w, dt in widths],
        compiler_params=_cparams("parallel"),
        name="mixin",
    )(x, gain, w_in, gq, gk, w2, ab)


def _mixout_body(x_ref, oa_ref, ob_ref, oc_ref, wo_ref, g_ref, wg_ref, wu_ref, wd_ref, o_ref):
    mix = jnp.dot(oa_ref[...].astype(BF16), wo_ref[0:A_WIDTH, :], preferred_element_type=F32)
    mix = mix + jnp.dot(ob_ref[...].astype(BF16), wo_ref[A_WIDTH:A_WIDTH + B_WIDTH, :],
                        preferred_element_type=F32)
    mix = mix + jnp.dot(oc_ref[...].astype(BF16), wo_ref[A_WIDTH + B_WIDTH:, :],
                        preferred_element_type=F32)
    o_ref[...] = _ffn_math(x_ref[...] + mix, g_ref[...], wg_ref, wu_ref, wd_ref)


def _mixout_call(x, oa, ob, oc, w_out, gain, wg, wu, wd):
    n = x.shape[0]
    tm = TOKEN_TILE
    row = lambda i: (i, 0)
    return pl.pallas_call(
        _mixout_body,
        grid=(n // tm,),
        in_specs=[pl.BlockSpec((tm, D_MODEL), row), pl.BlockSpec((tm, A_WIDTH), row),
                  pl.BlockSpec((tm, B_WIDTH), row), pl.BlockSpec((tm, C_WIDTH), row),
                  _const_spec((D_MODEL, D_MODEL)), _const_spec((1, D_MODEL)),
                  _const_spec((D_MODEL, D_FF)), _const_spec((D_MODEL, D_FF)),
                  _const_spec((D_FF, D_MODEL))],
        out_specs=pl.BlockSpec((tm, D_MODEL), row),
        out_shape=jax.ShapeDtypeStruct((n, D_MODEL), F32),
        compiler_params=_cparams("parallel"),
        name="mixout_ffn",
    )(x, oa, ob, oc, w_out, gain, wg, wu, wd)


def _alibi_slope(h):
    return 2.0 ** (-8.0 * (h + 1) / A_HEADS)


def _diff_lambda(lv_ref, lam_init):
    lv = lv_ref[...]
    s01 = jnp.sum(lv[0:1, :] * lv[1:2, :], axis=1, keepdims=True)
    s23 = jnp.sum(lv[2:3, :] * lv[3:4, :], axis=1, keepdims=True)
    return jnp.exp(s01) - jnp.exp(s23) + lam_init


def _attn_prompt_body(qt_ref, k_ref, vt_ref, lv_ref, go_ref, o_ref,
                      qx_sc, bias_sc, sa_sc, sb_sc, m_sc, l_sc, acc_sc, *, lam_init):
    t = ATT_TILE
    qi = pl.program_id(1)
    lam = _diff_lambda(lv_ref, lam_init)
    sub = lax.broadcasted_iota(jnp.int32, (LANE, t), 0)
    krow = lax.broadcasted_iota(jnp.int32, (t, 2 * t), 0)
    qcol = lax.broadcasted_iota(jnp.int32, (t, 2 * t), 1)
    causal = krow <= jnp.where(qcol >= t, qcol - t, qcol)
    kloc = krow.astype(F32)

    heads = [(h, slice(h * LANE, (h + 1) * LANE), _alibi_slope(h)) for h in range(A_HEADS)]

    for h, hs, slope in heads:
        qt = qt_ref[hs, :]
        zero = jnp.zeros_like(qt)
        qx_sc[h] = jnp.concatenate([jnp.where(sub < A_QK_DIM, qt, zero),
                                    jnp.where(sub >= A_QK_DIM, qt, zero)], axis=1)
        bias_sc[h] = slope * kloc
        m_sc[h] = jnp.full((1, 2 * t), -jnp.inf, F32)
        l_sc[h] = jnp.zeros((1, 2 * t), F32)
        acc_sc[h] = jnp.zeros((LANE, 2 * t), F32)

    def scores(kj, s_ref):
        k0 = pl.multiple_of(kj * t, t)
        for h, hs, _ in heads:
            s_ref[h] = jnp.dot(k_ref[pl.ds(k0, t), hs], qx_sc[h], preferred_element_type=F32)

    def update(kj, s_ref, masked):
        dist = ((kj - qi) * t).astype(F32)
        for h, hs, slope in heads:
            vt = vt_ref[kj, hs, :]
            s = s_ref[h] + bias_sc[h]
            if masked:
                s = jnp.where(causal, s, -jnp.inf)
            c = slope * dist
            m_old = m_sc[h]
            m_new = jnp.maximum(m_old, jnp.max(s, axis=0, keepdims=True) + c)
            alpha = jnp.exp(m_old - m_new)
            p = jnp.exp(s - (m_new - c))
            l_sc[h] = alpha * l_sc[h] + jnp.sum(p, axis=0, keepdims=True)
            acc_sc[h] = alpha * acc_sc[h] + jnp.dot(vt, p.astype(BF16), preferred_element_type=F32)
            m_sc[h] = m_new

    def pair(i, carry):
        kj = 2 * i
        scores(kj + 1, sb_sc)
        update(kj, sa_sc, False)
        scores(kj + 2, sa_sc)
        update(kj + 1, sb_sc, False)
        return carry

    scores(0, sa_sc)
    lax.fori_loop(0, qi // 2, pair, 0)
    odd = qi % 2 == 1

    @pl.when(odd)
    def _():
        scores(qi, sb_sc)
        update(qi - 1, sa_sc, False)
        update(qi, sb_sc, True)

    @pl.when(jnp.logical_not(odd))
    def _():
        update(qi, sa_sc, True)

    for h, hs, _ in heads:
        o = acc_sc[h] / l_sc[h]
        od = o[:, :t] - lam * o[:, t:]
        ms = jnp.mean(od * od, axis=0, keepdims=True)
        on = od * lax.rsqrt(ms + RMS_EPS) * go_ref[...] * (1.0 - lam_init)
        o_ref[:, hs] = on.T


def _attn_prompt_call(qb, kb, vb, lv, go_col, lam_init, batch, seq):
    t = ATT_TILE
    nt = seq // t
    tiles_t = lambda a: jnp.swapaxes(a.reshape(batch, nt, t, A_WIDTH), 2, 3)
    return pl.pallas_call(
        functools.partial(_attn_prompt_body, lam_init=lam_init),
        grid=(batch, nt),
        in_specs=[pl.BlockSpec((None, None, A_WIDTH, t), lambda b, i: (b, i, 0, 0)),
                  pl.BlockSpec((seq, A_WIDTH), lambda b, i: (b, 0)),
                  pl.BlockSpec((None, nt, A_WIDTH, t), lambda b, i: (b, 0, 0, 0)),
                  _const_spec((4, A_QK_DIM)), _const_spec((A_V_DIM, 1))],
        out_specs=pl.BlockSpec((t, A_WIDTH), lambda b, i: (b * nt + i, 0)),
        out_shape=jax.ShapeDtypeStruct((batch * seq, A_WIDTH), F32),
        scratch_shapes=[pltpu.VMEM((A_HEADS, LANE, 2 * t), BF16), pltpu.VMEM((A_HEADS, t, 2 * t), F32),
                        pltpu.VMEM((A_HEADS, t, 2 * t), F32), pltpu.VMEM((A_HEADS, t, 2 * t), F32),
                        pltpu.VMEM((A_HEADS, 1, 2 * t), F32), pltpu.VMEM((A_HEADS, 1, 2 * t), F32),
                        pltpu.VMEM((A_HEADS, LANE, 2 * t), F32)],
        compiler_params=_cparams("parallel", "arbitrary"),
        name="attn_prompt",
    )(tiles_t(qb), kb, tiles_t(vb), lv, go_col)


def _attn_sample_body(pt_ref, q_ref, kn_ref, vn_ref, lv_ref, go_ref, *rest, lam_init, n_pages, n_new):
    del pt_ref
    k_pages = rest[:n_pages]
    v_pages = rest[n_pages:2 * n_pages]
    o_ref = rest[2 * n_pages]
    rows = SAMPLE_Q_ROWS
    past = n_pages * PAGE_SIZE
    lam = _diff_lambda(lv_ref, lam_init)
    row = lax.broadcasted_iota(jnp.int32, (rows, 1), 0)
    tok = jnp.bitwise_and(row, n_new - 1)
    qpos = past + tok
    lane = lax.broadcasted_iota(jnp.int32, (rows, LANE), 1)
    is_map1 = row < n_new
    is_map2 = jnp.logical_and(row >= n_new, row < 2 * n_new)
    kpos_past = lax.broadcasted_iota(jnp.int32, (1, past), 1)
    new_col = lax.broadcasted_iota(jnp.int32, (1, rows), 1)
    kpos_new = past + new_col

    for h in range(A_HEADS):
        hs = slice(h * LANE, (h + 1) * LANE)
        slope = _alibi_slope(h)
        qh = q_ref[:, hs].astype(F32)
        qx = jnp.where(lane < A_QK_DIM, jnp.where(is_map1, qh, 0.0), jnp.where(is_map2, qh, 0.0))
        qx = qx.astype(BF16)
        nt = (((1,), (1,)), ((), ()))
        s_parts = [jnp.dot(qx, k_pages[p][hs, :].astype(BF16), preferred_element_type=F32)
                   for p in range(n_pages)]
        s = jnp.concatenate(s_parts, axis=1)
        s = s - slope * (qpos - kpos_past).astype(F32)
        sn = lax.dot_general(qx, kn_ref[:, hs], nt, preferred_element_type=F32)
        sn = sn - slope * (qpos - kpos_new).astype(F32)
        sn = jnp.where(new_col <= tok, sn, -jnp.inf)
        m = jnp.maximum(jnp.max(s, axis=1, keepdims=True), jnp.max(sn, axis=1, keepdims=True))
        p = jnp.exp(s - m)
        pn = jnp.exp(sn - m)
        l = jnp.sum(p, axis=1, keepdims=True) + jnp.sum(pn, axis=1, keepdims=True)
        pb = p.astype(BF16)
        acc = jnp.dot(pn.astype(BF16), vn_ref[:, hs], preferred_element_type=F32)
        for pg in range(n_pages):
            acc = acc + jnp.dot(pb[:, pg * PAGE_SIZE:(pg + 1) * PAGE_SIZE],
                                v_pages[pg][pl.ds(h, PAGE_SIZE, stride=A_HEADS), :].astype(BF16),
                                preferred_element_type=F32)
        o = acc / l
        od = o[0:n_new] - lam * o[n_new:2 * n_new]
        o_ref[:, hs] = _rms(od, go_ref[...]) * (1.0 - lam_init)


def _attn_sample_call(page_table, qx, kn, vn, lv, go, cache_kt, cache_vr, layer, lam_init):
    batch, n_pages = page_table.shape
    n_new = 4
    rows = SAMPLE_Q_ROWS
    tok_spec = pl.BlockSpec((None, rows, A_WIDTH), lambda b, pt: (b, 0, 0))

    def k_spec(p):
        return pl.BlockSpec((None, None, A_WIDTH, PAGE_SIZE), lambda b, pt: (layer, pt[b, p], 0, 0))

    def v_spec(p):
        return pl.BlockSpec((None, None, PAGE_SIZE * A_HEADS, A_V_DIM),
                            lambda b, pt: (layer, pt[b, p], 0, 0))

    const = lambda shape: pl.BlockSpec(shape, lambda b, pt: (0, 0))
    grid_spec = pltpu.PrefetchScalarGridSpec(
        num_scalar_prefetch=1,
        grid=(batch,),
        in_specs=([tok_spec, tok_spec, tok_spec, const((4, A_QK_DIM)), const((1, A_V_DIM))]
                  + [k_spec(p) for p in range(n_pages)]
                  + [v_spec(p) for p in range(n_pages)]),
        out_specs=pl.BlockSpec((None, n_new, A_WIDTH), lambda b, pt: (b, 0, 0)),
    )
    return pl.pallas_call(
        functools.partial(_attn_sample_body, lam_init=lam_init, n_pages=n_pages, n_new=n_new),
        grid_spec=grid_spec,
        out_shape=jax.ShapeDtypeStruct((batch, n_new, A_WIDTH), F32),
        compiler_params=_cparams("parallel"),
        name="attn_sample",
    )(page_table, qx, kn, vn, lv, go, *([cache_kt] * n_pages), *([cache_vr] * n_pages))


def _expm1(x):
    u = jnp.exp(x)
    return jnp.where(u == 1.0, x, (u - 1.0) * x / jnp.log(u))


def _rglru_gates(xc, wa_ref, wx_ref, ba, bx, lam):
    xb = xc.astype(BF16)
    r = jax.nn.sigmoid(jnp.dot(xb, wa_ref[...], preferred_element_type=F32) + ba)
    i = jax.nn.sigmoid(jnp.dot(xb, wx_ref[...], preferred_element_type=F32) + bx)
    log_a = (-RG_C * _softplus(-lam)) * r
    a = jnp.exp(log_a)
    u = jnp.sqrt(-_expm1(2.0 * log_a)) * (i * xc)
    return a, u


def _scan_rows(a, b):
    n = a.shape[0]
    row = lax.broadcasted_iota(jnp.int32, a.shape, 0)
    s = 1
    while s < n:
        keep = row >= s
        b = jnp.where(keep, a * pltpu.roll(b, s, axis=0) + b, b)
        a = jnp.where(keep, a * pltpu.roll(a, s, axis=0), a)
        s *= 2
    return a, b


def _rglru_prompt_body(x_ref, gate_ref, c0_ref, h0_ref, cw_ref, cb_ref, wa_ref, wx_ref,
                       ba_ref, bx_ref, lam_ref, o_ref, tail_ref, hl_ref, *, seq):
    t = SCAN_TILE
    cw = cw_ref[...]
    cb, ba, bx, lam = cb_ref[...], ba_ref[...], bx_ref[...], lam_ref[...]

    def block(blk, h):
        t0 = pl.multiple_of(blk * t, t)
        xb = x_ref[pl.ds(t0, t), :]
        tp = pl.multiple_of(jnp.maximum(t0 - SUBLANE, 0), SUBLANE)
        prev = jnp.where(blk == 0, c0_ref[...], x_ref[pl.ds(tp, SUBLANE), :])
        xx = jnp.concatenate([prev, xb], axis=0)
        xc = cb
        for j in range(CONV_W):
            d = CONV_W - 1 - j
            sh = xb if d == 0 else pltpu.roll(xx, d, axis=0)[SUBLANE:SUBLANE + t]
            xc = xc + cw[j:j + 1, :] * sh
        a, u = _rglru_gates(xc, wa_ref, wx_ref, ba, bx, lam)
        pa, hb = _scan_rows(a, u)
        hs = hb + pa * h
        o_ref[pl.ds(t0, t), :] = hs * gate_ref[pl.ds(t0, t), :]
        return hs[t - 1:t, :]

    h_last = lax.fori_loop(0, seq // t, block, h0_ref[...])
    hl_ref[...] = h_last
    tail_ref[...] = x_ref[seq - SUBLANE:seq, :]


def _rglru_prompt_call(bx, gate, c0, h0, cw, cb, wa, wx, ba, bxb, lam, batch, seq):
    seq_spec = pl.BlockSpec((seq, B_WIDTH), lambda b: (b, 0))
    vec = _const_spec((1, B_WIDTH))
    return pl.pallas_call(
        functools.partial(_rglru_prompt_body, seq=seq),
        grid=(batch,),
        in_specs=[seq_spec, seq_spec,
                  pl.BlockSpec((None, SUBLANE, B_WIDTH), lambda b: (b, 0, 0)),
                  pl.BlockSpec((None, 1, B_WIDTH), lambda b: (b, 0, 0)),
                  _const_spec((CONV_W, B_WIDTH)), vec,
                  _const_spec((B_WIDTH, B_WIDTH)), _const_spec((B_WIDTH, B_WIDTH)), vec, vec, vec],
        out_specs=[seq_spec,
                   pl.BlockSpec((None, SUBLANE, B_WIDTH), lambda b: (b, 0, 0)),
                   pl.BlockSpec((None, 1, B_WIDTH), lambda b: (b, 0, 0))],
        out_shape=[jax.ShapeDtypeStruct((batch * seq, B_WIDTH), F32),
                   jax.ShapeDtypeStruct((batch, SUBLANE, B_WIDTH), F32),
                   jax.ShapeDtypeStruct((batch, 1, B_WIDTH), F32)],
        compiler_params=_cparams("parallel"),
        name="rglru_prompt",
    )(bx, gate, c0, h0, cw, cb, wa, wx, ba, bxb, lam)


def _rglru_sample_body(x_ref, gate_ref, c0_ref, h0_ref, cw_ref, cb_ref, wa_ref, wx_ref,
                       ba_ref, bx_ref, lam_ref, o_ref, cn_ref, hl_ref, *, steps):
    cw = cw_ref[...]
    cb, ba, bx, lam = cb_ref[...], ba_ref[...], bx_ref[...], lam_ref[...]
    xs = [c0_ref[j] for j in range(CONV_W - 1)] + [x_ref[s] for s in range(steps)]
    h = h0_ref[...]
    for s in range(steps):
        xc = cb
        for j in range(CONV_W):
            xc = xc + cw[j:j + 1, :] * xs[s + j]
        a, u = _rglru_gates(xc, wa_ref, wx_ref, ba, bx, lam)
        h = a * h + u
        o_ref[s] = h * gate_ref[s]
    for j in range(CONV_W - 1):
        cn_ref[j] = xs[steps + j]
    hl_ref[...] = h


def _rglru_sample_call(x_tm, gate_tm, c0_tm, h0, cw, cb, wa, wx, ba, bxb, lam):
    steps, batch, _ = x_tm.shape
    return pl.pallas_call(
        functools.partial(_rglru_sample_body, steps=steps),
        out_shape=[jax.ShapeDtypeStruct((steps, batch, B_WIDTH), F32),
                   jax.ShapeDtypeStruct((CONV_W - 1, batch, B_WIDTH), F32),
                   jax.ShapeDtypeStruct((batch, B_WIDTH), F32)],
        name="rglru_sample",
    )(x_tm, gate_tm, c0_tm, h0, cw, cb, wa, wx, ba, bxb, lam)


def _cumsum_rows(x):
    n = x.shape[0]
    row = lax.broadcasted_iota(jnp.int32, x.shape, 0)
    s = 1
    while s < n:
        x = x + jnp.where(row >= s, pltpu.roll(x, s, axis=0), 0.0)
        s *= 2
    return x


def _gla_body(q_ref, k_ref, v_ref, g_ref, gate_ref, gain_ref, s0_ref, o_ref, st_ref, s_sc,
              *, seqs, block, sub):
    tb = pl.program_id(1)
    rt = lax.shift_right_logical(lax.broadcasted_iota(jnp.int32, (C_WIDTH, C_QK), 0), 6)
    ct = lax.shift_right_logical(lax.broadcasted_iota(jnp.int32, (C_WIDTH, C_QK), 1), 5)
    diag = jnp.where(rt == ct, 1.0, 0.0)

    @pl.when(tb == 0)
    def _():
        for sq in range(seqs):
            s_in = jnp.concatenate([s0_ref[sq], jnp.zeros((C_QK, LANE - C_DV), F32)], axis=1)
            s_t = s_in.T[:C_DV, :]
            s_sc[sq] = jnp.concatenate([s_t] * C_HEADS, axis=0) * diag

    r = lax.shift_right_logical(lax.broadcasted_iota(jnp.int32, (C_QK, C_WIDTH), 0), 5)
    c = lax.shift_right_logical(lax.broadcasted_iota(jnp.int32, (C_QK, C_WIDTH), 1), 6)
    expand = jnp.where(r == c, 1.0, 0.0).astype(BF16)
    gmat = _group_matrix(C_WIDTH, C_DV, 1.0 / C_DV, BF16)
    ii = lax.broadcasted_iota(jnp.int32, (sub, sub, C_QK), 0)
    jj = lax.broadcasted_iota(jnp.int32, (sub, sub, C_QK), 1)

    for sq, ci in [(sq, ci) for sq in range(seqs) for ci in range(block // sub)]:
        rows = slice(sq * block + ci * sub, sq * block + (ci + 1) * sub)
        q, k, v = q_ref[rows, :], k_ref[rows, :], v_ref[rows, :]
        b = _cumsum_rows(g_ref[rows, :])
        st = s_sc[sq]
        o_inter = lax.dot_general((q * jnp.exp(b)).astype(BF16), st.astype(BF16),
                                  (((1,), (1,)), ((), ())), preferred_element_type=F32)
        diff = b[:, None, :] - b[None, :, :]
        decay = jnp.exp(jnp.where(jj <= ii, diff, -jnp.inf))
        prod = decay * q[:, None, :] * k[None, :, :]
        att = jnp.dot(prod.reshape(sub * sub, C_QK).astype(BF16), expand, preferred_element_type=F32)
        o_intra = jnp.sum(att.reshape(sub, sub, C_WIDTH) * v[None, :, :], axis=1)
        o = o_inter + o_intra
        b_last = b[sub - 1:sub, :]
        kd = (k * jnp.exp(b_last - b)).astype(BF16)
        upd = lax.dot_general(v.astype(BF16), kd, (((0,), (0,)), ((), ())), preferred_element_type=F32)
        s_sc[sq] = jnp.exp(b_last) * st + upd * diag
        on = o * lax.rsqrt(_group_mean(o * o, gmat) + RMS_EPS) * gain_ref[...]
        o_ref[rows, :] = on * gate_ref[rows, :]

    @pl.when(tb == pl.num_programs(1) - 1)
    def _():
        for sq in range(seqs):
            st = s_sc[sq]
            s_t = st[0:C_DV]
            for h in range(1, C_HEADS):
                s_t = s_t + st[h * C_DV:(h + 1) * C_DV]
            s_out = jnp.concatenate([s_t, jnp.zeros((LANE - C_DV, C_QK), F32)], axis=0).T
            st_ref[sq] = s_out[:, :C_DV]


def _gla_call(cq, ck, cv, ga, gate, gain, s0, batch, seq, block, sub, seqs=1):
    nb = seq // block
    assert seqs == 1 or nb == 1
    tok = lambda w: pl.BlockSpec((seqs * block, w), lambda b, i: (b * nb + i, 0))
    st_spec = pl.BlockSpec((seqs, C_QK, C_DV), lambda b, i: (b, 0, 0))
    return pl.pallas_call(
        functools.partial(_gla_body, seqs=seqs, block=block, sub=sub),
        grid=(batch // seqs, nb),
        in_specs=[tok(C_QK), tok(C_QK), tok(C_WIDTH), tok(C_QK), tok(C_WIDTH),
                  _const_spec((1, C_WIDTH)), st_spec],
        out_specs=[tok(C_WIDTH), st_spec],
        out_shape=[jax.ShapeDtypeStruct((batch * seq, C_WIDTH), F32),
                   jax.ShapeDtypeStruct((batch, C_QK, C_DV), F32)],
        scratch_shapes=[pltpu.VMEM((seqs, C_WIDTH, C_QK), F32)],
        compiler_params=_cparams("parallel", "arbitrary"),
        name="gla",
    )(cq, ck, cv, ga, gate, gain, s0)


def _layer_weights(l, w):
    row = lambda v: v.reshape(1, -1).astype(F32)
    pad_cols = D_IN_PAD - D_IN
    blockdiag = lambda m: jax.scipy.linalg.block_diag(*[m[i] for i in range(B_BLOCKS)]).astype(BF16)
    return dict(
        ffn1=(row(w['ffn1_norm'][l]), w['ffn1_w_gate'][l].astype(BF16), w['ffn1_w_up'][l].astype(BF16),
              w['ffn1_w_down'][l].astype(BF16)),
        ffn2=(row(w['ffn2_norm'][l]), w['ffn2_w_gate'][l].astype(BF16), w['ffn2_w_up'][l].astype(BF16),
              w['ffn2_w_down'][l].astype(BF16)),
        mix_norm=row(w['mix_norm'][l]),
        w_in=jnp.pad(w['w_in'][l], ((0, 0), (0, pad_cols))).astype(BF16),
        w_out=w['w_out'][l].astype(BF16),
        gq=row(jnp.tile(w['a_q_norm'][l], MXU_WIDTH // A_QK_DIM)),
        gk=row(jnp.tile(w['a_k_norm'][l], MXU_WIDTH // A_QK_DIM)),
        lv=w['a_lambda'][l].astype(F32),
        go=row(w['a_out_norm'][l]),
        go_col=w['a_out_norm'][l].reshape(-1, 1).astype(F32),
        cw=w['b_conv_w'][l].astype(F32),
        cb=row(w['b_conv_b'][l]),
        wa=blockdiag(w['b_gate_a_w'][l]),
        wx=blockdiag(w['b_gate_x_w'][l]),
        ba=row(w['b_gate_a_b'][l]),
        bxb=row(w['b_gate_x_b'][l]),
        lam=row(w['b_lambda'][l]),
        w2=jnp.pad(w['c_alpha_w2'][l], ((0, LANE - C_RANK), (0, 0))).astype(BF16),
        ab=row(w['c_alpha_b'][l]),
        gc=row(jnp.tile(w['c_out_norm'][l], C_HEADS)),
    )


def _prompt_layer(x, p, lam_init, batch, seq):
    x1 = _ffn_call(x, *p['ffn1'])
    qb, k, kb, v, vb, bx, bg, cq, ck, cv, cg, ga = _mixin_call(
        x1, p['mix_norm'], p['w_in'], p['gq'], p['gk'], p['w2'], p['ab'])
    oa = _attn_prompt_call(qb, kb, vb, p['lv'], p['go_col'], lam_init, batch, seq)
    ob, tail, h_last = _rglru_prompt_call(
        bx, bg, jnp.zeros((batch, SUBLANE, B_WIDTH), F32), jnp.zeros((batch, 1, B_WIDTH), F32),
        p['cw'], p['cb'], p['wa'], p['wx'], p['ba'], p['bxb'], p['lam'], batch, seq)
    oc, st = _gla_call(cq, ck, cv, ga, cg, p['gc'], jnp.zeros((batch, C_QK, C_DV), F32),
                       batch, seq, GLA_BLOCK, GLA_SUB)
    x2 = _mixout_call(x1, oa, ob, oc, p['w_out'], *p['ffn2'])
    state = (k.reshape(batch, seq, A_HEADS, 2, A_QK_DIM), v.reshape(batch, seq, A_HEADS, A_V_DIM),
             tail[:, SUBLANE - (CONV_W - 1):, :], h_last[:, 0, :],
             st.reshape(batch, C_HEADS, C_DK, C_DV))
    return x2, state


def _sample_layer(x, p, lam_init, layer, batch, steps, page_table, cache_k, cache_v, conv0, h0, s0):
    x1 = _ffn_call(x, *p['ffn1'])
    qb, k, kb, v, vb, bx, bg, cq, ck, cv, cg, ga = _mixin_call(
        x1, p['mix_norm'], p['w_in'], p['gq'], p['gk'], p['w2'], p['ab'])

    seqd = lambda a: a.reshape(batch, steps, a.shape[-1])
    q3 = seqd(qb)
    pad_rows = lambda a: jnp.pad(a, ((0, 0), (0, SAMPLE_Q_ROWS - a.shape[1]), (0, 0)))
    qx = pad_rows(jnp.concatenate([q3, q3], axis=1))
    oa = _attn_sample_call(page_table, qx, pad_rows(seqd(kb)), pad_rows(seqd(vb)), p['lv'], p['go'],
                           cache_k, cache_v, layer, lam_init)
    oa = oa.reshape(batch * steps, A_WIDTH)

    tm = lambda a: jnp.swapaxes(seqd(a), 0, 1)
    ob_tm, cn_tm, h_last = _rglru_sample_call(
        tm(bx), tm(bg), jnp.swapaxes(conv0.astype(F32), 0, 1), h0.astype(F32),
        p['cw'], p['cb'], p['wa'], p['wx'], p['ba'], p['bxb'], p['lam'])
    ob = jnp.swapaxes(ob_tm, 0, 1).reshape(batch * steps, B_WIDTH)

    pad8 = lambda a: jnp.pad(seqd(a), ((0, 0), (0, SUBLANE - steps), (0, 0))).reshape(batch * SUBLANE, -1)
    oc_pad, st = _gla_call(pad8(cq), pad8(ck), pad8(cv), pad8(ga), pad8(cg), p['gc'],
                           s0.astype(F32).reshape(batch, C_QK, C_DV), batch, SUBLANE, SUBLANE, SUBLANE,
                           seqs=GLA_SAMPLE_SEQS)
    oc = oc_pad.reshape(batch, SUBLANE, C_WIDTH)[:, :steps].reshape(batch * steps, C_WIDTH)

    x2 = _mixout_call(x1, oa, ob, oc, p['w_out'], *p['ffn2'])
    state = (k.reshape(batch, steps, A_HEADS, 2, A_QK_DIM), v.reshape(batch, steps, A_HEADS, A_V_DIM),
             jnp.swapaxes(cn_tm, 0, 1), h_last, st.reshape(batch, C_HEADS, C_DK, C_DV))
    return x2, state


def kernel(x_prompt, x_sample, cache_k, cache_v, state_conv, state_rglru, state_gla, page_table,
           ffn1_norm, ffn1_w_gate, ffn1_w_up, ffn1_w_down, mix_norm, w_in, w_out,
           a_q_norm, a_k_norm, a_lambda, a_out_norm,
           b_conv_w, b_conv_b, b_gate_a_w, b_gate_a_b, b_gate_x_w, b_gate_x_b, b_lambda,
           c_alpha_w2, c_alpha_b, c_out_norm,
           ffn2_norm, ffn2_w_gate, ffn2_w_up, ffn2_w_down):
    weights = dict(
        ffn1_norm=ffn1_norm, ffn1_w_gate=ffn1_w_gate, ffn1_w_up=ffn1_w_up, ffn1_w_down=ffn1_w_down,
        mix_norm=mix_norm, w_in=w_in, w_out=w_out, a_q_norm=a_q_norm, a_k_norm=a_k_norm,
        a_lambda=a_lambda, a_out_norm=a_out_norm, b_conv_w=b_conv_w, b_conv_b=b_conv_b,
        b_gate_a_w=b_gate_a_w, b_gate_a_b=b_gate_a_b, b_gate_x_w=b_gate_x_w, b_gate_x_b=b_gate_x_b,
        b_lambda=b_lambda, c_alpha_w2=c_alpha_w2, c_alpha_b=c_alpha_b, c_out_norm=c_out_norm,
        ffn2_norm=ffn2_norm, ffn2_w_gate=ffn2_w_gate, ffn2_w_up=ffn2_w_up, ffn2_w_down=ffn2_w_down)
    bp, seq, _ = x_prompt.shape
    bs, steps, _ = x_sample.shape
    depth = ffn1_norm.shape[0]
    n_pool = cache_k.shape[1]
    ckt = jnp.transpose(cache_k, (0, 1, 3, 4, 5, 2)).reshape(depth, n_pool, A_WIDTH, PAGE_SIZE)
    cvh = cache_v.reshape(depth, n_pool, PAGE_SIZE * A_HEADS, A_V_DIM)

    yp = x_prompt.reshape(bp * seq, D_MODEL)
    ys = x_sample.reshape(bs * steps, D_MODEL)
    outs_p, outs_s = [], []
    for l in range(depth):
        lam_init = 0.8 - 0.6 * math.exp(-0.3 * l)
        p = _layer_weights(l, weights)
        yp, st_p = _prompt_layer(yp, p, lam_init, bp, seq)
        ys, st_s = _sample_layer(ys, p, lam_init, l, bs, steps, page_table, ckt, cvh,
                                 state_conv[l], state_rglru[l], state_gla[l])
        outs_p.append(st_p)
        outs_s.append(st_s)
    stack = lambda outs: [jnp.stack([o[i] for o in outs]) for i in range(5)]
    k_p, v_p, conv_p, h_p, s_p = stack(outs_p)
    k_s, v_s, conv_s, h_s, s_s = stack(outs_s)
    return (yp.reshape(bp, seq, D_MODEL), ys.reshape(bs, steps, D_MODEL),
            k_p, v_p, conv_p, h_p, s_p, k_s, v_s, conv_s, h_s, s_s)
```

```python
import functools
import math

import jax
import jax.numpy as jnp
from jax import lax
from jax.experimental import pallas as pl
from jax.experimental.pallas import tpu as pltpu

F32 = jnp.float32
BF16 = jnp.bfloat16

D_MODEL = 1024
DEPTH = 4
PAGE_SIZE = 128
A_HEADS = 4
A_V_DIM = 128
A_QK_DIM = 64
A_WIDTH = A_HEADS * A_V_DIM
B_WIDTH = 256
B_BLOCKS = 4
CONV_W = 4
RG_C = 8.0
C_WIDTH = 256
C_HEADS = 4
C_DV = 64
C_DK = 32
C_QK = C_HEADS * C_DK
C_RANK = 16
GLA_TAU = 16.0
D_FF = 2816
RMS_EPS = 1e-6

OFF_Q, OFF_K, OFF_V, OFF_BX, OFF_C = 0, 512, 1024, 1536, 2048
D_IN = 2832
D_IN_PAD = 2944

LANE = 128
SUBLANE = 8
MXU_WIDTH = 256
VMEM_LIMIT_BYTES = 56 * 1024 * 1024

TOKEN_TILE = 512
FF_CHUNK = D_FF // 2
ATT_TILE = 256
ACC_EXTRA_ROWS = 16
SCAN_TILE = 128
GLA_BLOCK = 256
GLA_SUB = 16
GLA_SAMPLE_SEQS = 8
SAMPLE_Q_ROWS = 16


def _cparams(*sem):
    return pltpu.CompilerParams(dimension_semantics=sem, vmem_limit_bytes=VMEM_LIMIT_BYTES)


def _const_spec(shape):
    zeros = (0,) * len(shape)
    return pl.BlockSpec(shape, lambda *_: zeros, pipeline_mode=pl.Buffered(1))


def _rms(x, gain):
    return x * lax.rsqrt(jnp.mean(x * x, axis=-1, keepdims=True) + RMS_EPS) * gain


def _softplus(y):
    return jnp.maximum(y, 0.0) + jnp.log1p(jnp.exp(-jnp.abs(y)))


def _gelu_tanh(x):
    return x * (0.5 * (1.0 + jnp.tanh(math.sqrt(2.0 / math.pi) * (x + 0.044715 * (x * x * x)))))


def _silu(x):
    return x * jax.nn.sigmoid(x)


def _group_matrix(n, group, value, dtype):
    shift = group.bit_length() - 1
    r = lax.shift_right_logical(lax.broadcasted_iota(jnp.int32, (n, n), 0), shift)
    c = lax.shift_right_logical(lax.broadcasted_iota(jnp.int32, (n, n), 1), shift)
    return jnp.where(r == c, value, 0.0).astype(dtype)


def _group_mean(sq, gmat):
    return jnp.dot(sq.astype(BF16), gmat, preferred_element_type=F32)


def _ffn_math(x, gain, wg_ref, wu_ref, wd_ref):
    hb = _rms(x, gain).astype(BF16)
    acc = x
    for c in range(D_FF // FF_CHUNK):
        sl = slice(c * FF_CHUNK, (c + 1) * FF_CHUNK)
        a = jnp.dot(hb, wg_ref[:, sl], preferred_element_type=F32)
        u = jnp.dot(hb, wu_ref[:, sl], preferred_element_type=F32)
        t = (_silu(a) * u).astype(BF16)
        acc = acc + 0.5 * jnp.dot(t, wd_ref[sl, :], preferred_element_type=F32)
    return acc


def _ffn_body(x_ref, g_ref, wg_ref, wu_ref, wd_ref, o_ref):
    o_ref[...] = _ffn_math(x_ref[...], g_ref[...], wg_ref, wu_ref, wd_ref)


def _ffn_call(x, gain, wg, wu, wd):
    n = x.shape[0]
    tm = TOKEN_TILE
    row = lambda i: (i, 0)
    return pl.pallas_call(
        _ffn_body,
        grid=(n // tm,),
        in_specs=[pl.BlockSpec((tm, D_MODEL), row), _const_spec((1, D_MODEL)),
                  _const_spec((D_MODEL, D_FF)), _const_spec((D_MODEL, D_FF)),
                  _const_spec((D_FF, D_MODEL))],
        out_specs=pl.BlockSpec((tm, D_MODEL), row),
        out_shape=jax.ShapeDtypeStruct((n, D_MODEL), F32),
        compiler_params=_cparams("parallel"),
        name="ffn",
    )(x, gain, wg, wu, wd)


def _mixin_body(x_ref, g_ref, win_ref, gq_ref, gk_ref, w2_ref, ab_ref,
                q_ref, k_ref, kb_ref, v_ref, vb_ref, bx_ref, bg_ref,
                cq_ref, ck_ref, cv_ref, cg_ref, ga_ref, *, tiles_t):
    hb = _rms(x_ref[...], g_ref[...]).astype(BF16)
    gmat = _group_matrix(MXU_WIDTH, A_QK_DIM, 1.0 / A_QK_DIM, BF16)
    t = ATT_TILE

    def qk_norm(z, gain):
        return z * lax.rsqrt(_group_mean(z * z, gmat) + RMS_EPS) * gain

    def put(ref, sl, val):
        if tiles_t:
            for u in range(val.shape[0] // t):
                ref[u, sl, :] = val[u * t:(u + 1) * t, :].T.astype(BF16)
        else:
            ref[:, sl] = val.astype(BF16)

    for c in range(A_WIDTH // MXU_WIDTH):
        sl = slice(c * MXU_WIDTH, (c + 1) * MXU_WIDTH)
        zq = jnp.dot(hb, win_ref[:, OFF_Q + c * MXU_WIDTH:OFF_Q + (c + 1) * MXU_WIDTH],
                     preferred_element_type=F32)
        put(q_ref, sl, qk_norm(zq, gq_ref[...]) * (A_QK_DIM ** -0.5))
        zk = jnp.dot(hb, win_ref[:, OFF_K + c * MXU_WIDTH:OFF_K + (c + 1) * MXU_WIDTH],
                     preferred_element_type=F32)
        kn = qk_norm(zk, gk_ref[...])
        if tiles_t:
            k_ref[sl, :] = kn.T
        else:
            k_ref[:, sl] = kn
        kb_ref[:, sl] = kn.astype(BF16)
        zv = jnp.dot(hb, win_ref[:, OFF_V + c * MXU_WIDTH:OFF_V + (c + 1) * MXU_WIDTH],
                     preferred_element_type=F32)
        if tiles_t:
            for hh in range(MXU_WIDTH // A_V_DIM):
                head = c * (MXU_WIDTH // A_V_DIM) + hh
                v_ref[pl.ds(head, zv.shape[0], stride=A_HEADS), :] = zv[:, hh * A_V_DIM:(hh + 1) * A_V_DIM]
        else:
            v_ref[:, sl] = zv
        put(vb_ref, sl, zv)
    zb = jnp.dot(hb, win_ref[:, OFF_BX:OFF_BX + 2 * B_WIDTH], preferred_element_type=F32)
    bx_ref[...] = zb[:, :B_WIDTH]
    bg_ref[...] = _gelu_tanh(zb[:, B_WIDTH:])
    zc = jnp.dot(hb, win_ref[:, OFF_C:D_IN_PAD], preferred_element_type=F32)
    cq_ref[...] = zc[:, 0:C_QK] * (C_DK ** -0.5)
    ck_ref[...] = zc[:, C_QK:2 * C_QK]
    cv_ref[...] = zc[:, 2 * C_QK:2 * C_QK + C_WIDTH]
    cg_ref[...] = _silu(zc[:, 2 * C_QK + C_WIDTH:2 * C_QK + 2 * C_WIDTH])
    za = zc[:, 2 * C_QK + 2 * C_WIDTH:].astype(BF16)
    pre = jnp.dot(za, w2_ref[...], preferred_element_type=F32) + ab_ref[...]
    ga_ref[...] = -_softplus(-pre) * (1.0 / GLA_TAU)


def _mixin_call(x, gain, w_in, gq, gk, w2, ab, tiles_t=False, seq=None):
    n = x.shape[0]
    tm = TOKEN_TILE
    t = ATT_TILE
    row = lambda i: (i, 0)
    widths = [None, (A_WIDTH, F32), (A_WIDTH, BF16), (A_WIDTH, F32), None,
              (B_WIDTH, F32), (B_WIDTH, F32),
              (C_QK, F32), (C_QK, F32), (C_WIDTH, F32), (C_WIDTH, F32), (C_QK, F32)]
    if tiles_t:
        qv_spec = pl.BlockSpec((tm // t, A_WIDTH, t), lambda i: (i, 0, 0))
        qv_shape = jax.ShapeDtypeStruct((n // t, A_WIDTH, t), BF16)
        per_seq = seq // tm
        k_spec = pl.BlockSpec((None, A_WIDTH, tm), lambda i: (i // per_seq, 0, i % per_seq))
        k_shape = jax.ShapeDtypeStruct((n // seq, A_WIDTH, seq), F32)
        v_spec = pl.BlockSpec((tm * A_HEADS, A_V_DIM), row)
        v_shape = jax.ShapeDtypeStruct((n * A_HEADS, A_V_DIM), F32)
    else:
        v_spec = pl.BlockSpec((tm, A_WIDTH), row)
        v_shape = jax.ShapeDtypeStruct((n, A_WIDTH), F32)
        qv_spec = pl.BlockSpec((tm, A_WIDTH), row)
        qv_shape = jax.ShapeDtypeStruct((n, A_WIDTH), BF16)
        k_spec = pl.BlockSpec((tm, A_WIDTH), row)
        k_shape = jax.ShapeDtypeStruct((n, A_WIDTH), F32)
    return pl.pallas_call(
        functools.partial(_mixin_body, tiles_t=tiles_t),
        grid=(n // tm,),
        in_specs=[pl.BlockSpec((tm, D_MODEL), row), _const_spec((1, D_MODEL)),
                  _const_spec((D_MODEL, D_IN_PAD)), _const_spec((1, MXU_WIDTH)),
                  _const_spec((1, MXU_WIDTH)), _const_spec((LANE, C_QK)), _const_spec((1, C_QK))],
        out_specs=[qv_spec if w is None else k_spec if i == 1 else v_spec if i == 3
                   else pl.BlockSpec((tm, w[0]), row) for i, w in enumerate(widths)],
        out_shape=[qv_shape if w is None else k_shape if i == 1 else v_shape if i == 3
                   else jax.ShapeDtypeStruct((n, w[0]), w[1]) for i, w in enumerate(widths)],
        compiler_params=_cparams("parallel"),
        name="mixin",
    )(x, gain, w_in, gq, gk, w2, ab)


def _mixout_body(x_ref, oa_ref, ob_ref, oc_ref, wo_ref, g_ref, wg_ref, wu_ref, wd_ref, o_ref):
    mix = jnp.dot(oa_ref[...].astype(BF16), wo_ref[0:A_WIDTH, :], preferred_element_type=F32)
    mix = mix + jnp.dot(ob_ref[...].astype(BF16), wo_ref[A_WIDTH:A_WIDTH + B_WIDTH, :],
                        preferred_element_type=F32)
    mix = mix + jnp.dot(oc_ref[...].astype(BF16), wo_ref[A_WIDTH + B_WIDTH:, :],
                        preferred_element_type=F32)
    o_ref[...] = _ffn_math(x_ref[...] + mix, g_ref[...], wg_ref, wu_ref, wd_ref)


def _mixout_call(x, oa, ob, oc, w_out, gain, wg, wu, wd):
    n = x.shape[0]
    tm = TOKEN_TILE
    row = lambda i: (i, 0)
    return pl.pallas_call(
        _mixout_body,
        grid=(n // tm,),
        in_specs=[pl.BlockSpec((tm, D_MODEL), row), pl.BlockSpec((tm, A_WIDTH), row),
                  pl.BlockSpec((tm, B_WIDTH), row), pl.BlockSpec((tm, C_WIDTH), row),
                  _const_spec((D_MODEL, D_MODEL)), _const_spec((1, D_MODEL)),
                  _const_spec((D_MODEL, D_FF)), _const_spec((D_MODEL, D_FF)),
                  _const_spec((D_FF, D_MODEL))],
        out_specs=pl.BlockSpec((tm, D_MODEL), row),
        out_shape=jax.ShapeDtypeStruct((n, D_MODEL), F32),
        compiler_params=_cparams("parallel"),
        name="mixout_ffn",
    )(x, oa, ob, oc, w_out, gain, wg, wu, wd)


def _alibi_slope(h):
    return 2.0 ** (-8.0 * (h + 1) / A_HEADS)


def _diff_lambda(lv_ref, lam_init):
    lv = lv_ref[...]
    s01 = jnp.sum(lv[0:1, :] * lv[1:2, :], axis=1, keepdims=True)
    s23 = jnp.sum(lv[2:3, :] * lv[3:4, :], axis=1, keepdims=True)
    return jnp.exp(s01) - jnp.exp(s23) + lam_init


def _attn_prompt_body(qt_ref, k_ref, vt_ref, lv_ref, go_ref, o_ref,
                      qx_sc, sa_sc, sb_sc, m_sc, acc_sc, *, lam_init):
    t = ATT_TILE
    qi = pl.program_id(1)
    lam = _diff_lambda(lv_ref, lam_init)
    sub = lax.broadcasted_iota(jnp.int32, (LANE, t), 0)
    krow = lax.broadcasted_iota(jnp.int32, (t, 2 * t), 0)
    qcol = lax.broadcasted_iota(jnp.int32, (t, 2 * t), 1)
    causal = krow <= jnp.where(qcol >= t, qcol - t, qcol)

    heads = [(h, slice(h * LANE, (h + 1) * LANE), _alibi_slope(h)) for h in range(A_HEADS)]

    arow = lax.broadcasted_iota(jnp.int32, (LANE, 2 * t), 0)
    klane = lax.broadcasted_iota(jnp.int32, (t, LANE), 1)
    koff = lax.broadcasted_iota(jnp.int32, (t, LANE), 0).astype(F32)
    key_offset = jnp.where(klane == 0, koff, 0.0).astype(BF16)
    ones_rows = jnp.ones((ACC_EXTRA_ROWS, t), BF16)

    for h, hs, slope in heads:
        qt = qt_ref[hs, :]
        zero = jnp.zeros_like(qt)
        slope_rows = jnp.where(arow == 0, slope, 0.0).astype(BF16)
        qx_sc[h] = jnp.concatenate(
            [jnp.concatenate([jnp.where(sub < A_QK_DIM, qt, zero),
                              jnp.where(sub >= A_QK_DIM, qt, zero)], axis=1), slope_rows], axis=0)
        m_sc[h] = jnp.full((1, 2 * t), -jnp.inf, F32)
        acc_sc[h] = jnp.zeros((LANE + ACC_EXTRA_ROWS, 2 * t), F32)

    def scores(kj, s_ref):
        k0 = pl.multiple_of(kj * t, t)
        for h, hs, _ in heads:
            ka = jnp.concatenate([k_ref[pl.ds(k0, t), hs], key_offset], axis=1)
            s_ref[h] = jnp.dot(ka, qx_sc[h], preferred_element_type=F32)

    def update(kj, s_ref, masked):
        dist = ((kj - qi) * t).astype(F32)
        for h, hs, slope in heads:
            va = jnp.concatenate([vt_ref[kj, hs, :], ones_rows], axis=0)
            s = s_ref[h]
            if masked:
                s = jnp.where(causal, s, -jnp.inf)
            c = slope * dist
            m_old = m_sc[h]
            m_new = jnp.maximum(m_old, jnp.max(s, axis=0, keepdims=True) + c)
            alpha = jnp.exp(m_old - m_new)
            p = jnp.exp(s - (m_new - c))
            acc_sc[h] = alpha * acc_sc[h] + jnp.dot(va, p.astype(BF16), preferred_element_type=F32)
            m_sc[h] = m_new

    def pair(i, carry):
        kj = 2 * i
        scores(kj + 1, sb_sc)
        update(kj, sa_sc, False)
        scores(kj + 2, sa_sc)
        update(kj + 1, sb_sc, False)
        return carry

    scores(0, sa_sc)
    lax.fori_loop(0, qi // 2, pair, 0)
    odd = qi % 2 == 1

    @pl.when(odd)
    def _():
        scores(qi, sb_sc)
        update(qi - 1, sa_sc, False)
        update(qi, sb_sc, True)

    @pl.when(jnp.logical_not(odd))
    def _():
        update(qi, sa_sc, True)

    for h, hs, _ in heads:
        acc = acc_sc[h]
        o = acc[:LANE] / acc[LANE:LANE + 1]
        od = o[:, :t] - lam * o[:, t:]
        ms = jnp.mean(od * od, axis=0, keepdims=True)
        on = od * lax.rsqrt(ms + RMS_EPS) * go_ref[...] * (1.0 - lam_init)
        o_ref[:, hs] = on.T


def _attn_prompt_call(qt, kb, vt, lv, go_col, lam_init, batch, seq):
    t = ATT_TILE
    nt = seq // t
    return pl.pallas_call(
        functools.partial(_attn_prompt_body, lam_init=lam_init),
        grid=(batch, nt),
        in_specs=[pl.BlockSpec((None, A_WIDTH, t), lambda b, i: (b * nt + i, 0, 0)),
                  pl.BlockSpec((seq, A_WIDTH), lambda b, i: (b, 0)),
                  pl.BlockSpec((nt, A_WIDTH, t), lambda b, i: (b, 0, 0)),
                  _const_spec((4, A_QK_DIM)), _const_spec((A_V_DIM, 1))],
        out_specs=pl.BlockSpec((t, A_WIDTH), lambda b, i: (b * nt + i, 0)),
        out_shape=jax.ShapeDtypeStruct((batch * seq, A_WIDTH), F32),
        scratch_shapes=[pltpu.VMEM((A_HEADS, 2 * LANE, 2 * t), BF16),
                        pltpu.VMEM((A_HEADS, t, 2 * t), F32), pltpu.VMEM((A_HEADS, t, 2 * t), F32),
                        pltpu.VMEM((A_HEADS, 1, 2 * t), F32),
                        pltpu.VMEM((A_HEADS, LANE + ACC_EXTRA_ROWS, 2 * t), F32)],
        compiler_params=_cparams("parallel", "arbitrary"),
        name="attn_prompt",
    )(qt, kb, vt, lv, go_col)


def _attn_sample_body(pt_ref, q_ref, kn_ref, vn_ref, lv_ref, go_ref, *rest, lam_init, n_pages, n_new):
    del pt_ref
    k_pages = rest[:n_pages]
    v_pages = rest[n_pages:2 * n_pages]
    o_ref = rest[2 * n_pages]
    rows = SAMPLE_Q_ROWS
    past = n_pages * PAGE_SIZE
    lam = _diff_lambda(lv_ref, lam_init)
    row = lax.broadcasted_iota(jnp.int32, (rows, 1), 0)
    tok = jnp.bitwise_and(row, n_new - 1)
    qpos = past + tok
    lane = lax.broadcasted_iota(jnp.int32, (rows, LANE), 1)
    is_map1 = row < n_new
    is_map2 = jnp.logical_and(row >= n_new, row < 2 * n_new)
    kpos_past = lax.broadcasted_iota(jnp.int32, (1, past), 1)
    new_col = lax.broadcasted_iota(jnp.int32, (1, rows), 1)
    kpos_new = past + new_col

    for h in range(A_HEADS):
        hs = slice(h * LANE, (h + 1) * LANE)
        slope = _alibi_slope(h)
        qh = q_ref[:, hs].astype(F32)
        qx = jnp.where(lane < A_QK_DIM, jnp.where(is_map1, qh, 0.0), jnp.where(is_map2, qh, 0.0))
        qx = qx.astype(BF16)
        nt = (((1,), (1,)), ((), ()))
        s_parts = [jnp.dot(qx, k_pages[p][hs, :].astype(BF16), preferred_element_type=F32)
                   for p in range(n_pages)]
        s = jnp.concatenate(s_parts, axis=1)
        s = s - slope * (qpos - kpos_past).astype(F32)
        sn = lax.dot_general(qx, kn_ref[:, hs], nt, preferred_element_type=F32)
        sn = sn - slope * (qpos - kpos_new).astype(F32)
        sn = jnp.where(new_col <= tok, sn, -jnp.inf)
        m = jnp.maximum(jnp.max(s, axis=1, keepdims=True), jnp.max(sn, axis=1, keepdims=True))
        p = jnp.exp(s - m)
        pn = jnp.exp(sn - m)
        l = jnp.sum(p, axis=1, keepdims=True) + jnp.sum(pn, axis=1, keepdims=True)
        pb = p.astype(BF16)
        acc = jnp.dot(pn.astype(BF16), vn_ref[:, hs], preferred_element_type=F32)
        for pg in range(n_pages):
            acc = acc + jnp.dot(pb[:, pg * PAGE_SIZE:(pg + 1) * PAGE_SIZE],
                                v_pages[pg][pl.ds(h, PAGE_SIZE, stride=A_HEADS), :].astype(BF16),
                                preferred_element_type=F32)
        o = acc / l
        od = o[0:n_new] - lam * o[n_new:2 * n_new]
        o_ref[:, hs] = _rms(od, go_ref[...]) * (1.0 - lam_init)


def _attn_sample_call(page_table, qx, kn, vn, lv, go, cache_kt, cache_vr, layer, lam_init):
    batch, n_pages = page_table.shape
    n_new = 4
    rows = SAMPLE_Q_ROWS
    tok_spec = pl.BlockSpec((None, rows, A_WIDTH), lambda b, pt: (b, 0, 0))

    def k_spec(p):
        return pl.BlockSpec((None, None, A_WIDTH, PAGE_SIZE), lambda b, pt: (layer, pt[b, p], 0, 0))

    def v_spec(p):
        return pl.BlockSpec((None, None, PAGE_SIZE * A_HEADS, A_V_DIM),
                            lambda b, pt: (layer, pt[b, p], 0, 0))

    const = lambda shape: pl.BlockSpec(shape, lambda b, pt: (0, 0))
    grid_spec = pltpu.PrefetchScalarGridSpec(
        num_scalar_prefetch=1,
        grid=(batch,),
        in_specs=([tok_spec, tok_spec, tok_spec, const((4, A_QK_DIM)), const((1, A_V_DIM))]
                  + [k_spec(p) for p in range(n_pages)]
                  + [v_spec(p) for p in range(n_pages)]),
        out_specs=pl.BlockSpec((None, n_new, A_WIDTH), lambda b, pt: (b, 0, 0)),
    )
    return pl.pallas_call(
        functools.partial(_attn_sample_body, lam_init=lam_init, n_pages=n_pages, n_new=n_new),
        grid_spec=grid_spec,
        out_shape=jax.ShapeDtypeStruct((batch, n_new, A_WIDTH), F32),
        compiler_params=_cparams("parallel"),
        name="attn_sample",
    )(page_table, qx, kn, vn, lv, go, *([cache_kt] * n_pages), *([cache_vr] * n_pages))


def _expm1(x):
    u = jnp.exp(x)
    return jnp.where(u == 1.0, x, (u - 1.0) * x / jnp.log(u))


def _rglru_gates(xc, wa_ref, wx_ref, ba, bx, lam):
    xb = xc.astype(BF16)
    r = jax.nn.sigmoid(jnp.dot(xb, wa_ref[...], preferred_element_type=F32) + ba)
    i = jax.nn.sigmoid(jnp.dot(xb, wx_ref[...], preferred_element_type=F32) + bx)
    log_a = (-RG_C * _softplus(-lam)) * r
    a = jnp.exp(log_a)
    u = jnp.sqrt(-_expm1(2.0 * log_a)) * (i * xc)
    return a, u


def _scan_rows(a, b):
    n = a.shape[0]
    row = lax.broadcasted_iota(jnp.int32, a.shape, 0)
    s = 1
    while s < n:
        keep = row >= s
        b = jnp.where(keep, a * pltpu.roll(b, s, axis=0) + b, b)
        a = jnp.where(keep, a * pltpu.roll(a, s, axis=0), a)
        s *= 2
    return a, b


def _rglru_prompt_body(x_ref, gate_ref, c0_ref, h0_ref, cw_ref, cb_ref, wa_ref, wx_ref,
                       ba_ref, bx_ref, lam_ref, o_ref, tail_ref, hl_ref, *, seq):
    t = SCAN_TILE
    cw = cw_ref[...]
    cb, ba, bx, lam = cb_ref[...], ba_ref[...], bx_ref[...], lam_ref[...]

    def block(blk, h):
        t0 = pl.multiple_of(blk * t, t)
        xb = x_ref[pl.ds(t0, t), :]
        tp = pl.multiple_of(jnp.maximum(t0 - SUBLANE, 0), SUBLANE)
        prev = jnp.where(blk == 0, c0_ref[...], x_ref[pl.ds(tp, SUBLANE), :])
        xx = jnp.concatenate([prev, xb], axis=0)
        xc = cb
        for j in range(CONV_W):
            d = CONV_W - 1 - j
            sh = xb if d == 0 else pltpu.roll(xx, d, axis=0)[SUBLANE:SUBLANE + t]
            xc = xc + cw[j:j + 1, :] * sh
        a, u = _rglru_gates(xc, wa_ref, wx_ref, ba, bx, lam)
        pa, hb = _scan_rows(a, u)
        hs = hb + pa * h
        o_ref[pl.ds(t0, t), :] = hs * gate_ref[pl.ds(t0, t), :]
        return hs[t - 1:t, :]

    h_last = lax.fori_loop(0, seq // t, block, h0_ref[...])
    hl_ref[...] = h_last
    tail_ref[...] = x_ref[seq - SUBLANE:seq, :]


def _rglru_prompt_call(bx, gate, c0, h0, cw, cb, wa, wx, ba, bxb, lam, batch, seq):
    seq_spec = pl.BlockSpec((seq, B_WIDTH), lambda b: (b, 0))
    vec = _const_spec((1, B_WIDTH))
    return pl.pallas_call(
        functools.partial(_rglru_prompt_body, seq=seq),
        grid=(batch,),
        in_specs=[seq_spec, seq_spec,
                  pl.BlockSpec((None, SUBLANE, B_WIDTH), lambda b: (b, 0, 0)),
                  pl.BlockSpec((None, 1, B_WIDTH), lambda b: (b, 0, 0)),
                  _const_spec((CONV_W, B_WIDTH)), vec,
                  _const_spec((B_WIDTH, B_WIDTH)), _const_spec((B_WIDTH, B_WIDTH)), vec, vec, vec],
        out_specs=[seq_spec,
                   pl.BlockSpec((None, SUBLANE, B_WIDTH), lambda b: (b, 0, 0)),
                   pl.BlockSpec((None, 1, B_WIDTH), lambda b: (b, 0, 0))],
        out_shape=[jax.ShapeDtypeStruct((batch * seq, B_WIDTH), F32),
                   jax.ShapeDtypeStruct((batch, SUBLANE, B_WIDTH), F32),
                   jax.ShapeDtypeStruct((batch, 1, B_WIDTH), F32)],
        compiler_params=_cparams("parallel"),
        name="rglru_prompt",
    )(bx, gate, c0, h0, cw, cb, wa, wx, ba, bxb, lam)


def _rglru_sample_body(x_ref, gate_ref, c0_ref, h0_ref, cw_ref, cb_ref, wa_ref, wx_ref,
                       ba_ref, bx_ref, lam_ref, o_ref, cn_ref, hl_ref, *, steps):
    cw = cw_ref[...]
    cb, ba, bx, lam = cb_ref[...], ba_ref[...], bx_ref[...], lam_ref[...]
    xs = [c0_ref[j] for j in range(CONV_W - 1)] + [x_ref[s] for s in range(steps)]
    h = h0_ref[...]
    for s in range(steps):
        xc = cb
        for j in range(CONV_W):
            xc = xc + cw[j:j + 1, :] * xs[s + j]
        a, u = _rglru_gates(xc, wa_ref, wx_ref, ba, bx, lam)
        h = a * h + u
        o_ref[s] = h * gate_ref[s]
    for j in range(CONV_W - 1):
        cn_ref[j] = xs[steps + j]
    hl_ref[...] = h


def _rglru_sample_call(x_tm, gate_tm, c0_tm, h0, cw, cb, wa, wx, ba, bxb, lam):
    steps, batch, _ = x_tm.shape
    return pl.pallas_call(
        functools.partial(_rglru_sample_body, steps=steps),
        out_shape=[jax.ShapeDtypeStruct((steps, batch, B_WIDTH), F32),
                   jax.ShapeDtypeStruct((CONV_W - 1, batch, B_WIDTH), F32),
                   jax.ShapeDtypeStruct((batch, B_WIDTH), F32)],
        name="rglru_sample",
    )(x_tm, gate_tm, c0_tm, h0, cw, cb, wa, wx, ba, bxb, lam)


def _cumsum_rows(x):
    n = x.shape[0]
    row = lax.broadcasted_iota(jnp.int32, x.shape, 0)
    s = 1
    while s < n:
        x = x + jnp.where(row >= s, pltpu.roll(x, s, axis=0), 0.0)
        s *= 2
    return x


def _gla_body(q_ref, k_ref, v_ref, g_ref, gate_ref, gain_ref, s0_ref, o_ref, st_ref, s_sc,
              *, seqs, block, sub):
    tb = pl.program_id(1)
    rt = lax.shift_right_logical(lax.broadcasted_iota(jnp.int32, (C_WIDTH, C_QK), 0), 6)
    ct = lax.shift_right_logical(lax.broadcasted_iota(jnp.int32, (C_WIDTH, C_QK), 1), 5)
    diag = jnp.where(rt == ct, 1.0, 0.0)

    @pl.when(tb == 0)
    def _():
        for sq in range(seqs):
            s_in = jnp.concatenate([s0_ref[sq], jnp.zeros((C_QK, LANE - C_DV), F32)], axis=1)
            s_t = s_in.T[:C_DV, :]
            s_sc[sq] = jnp.concatenate([s_t] * C_HEADS, axis=0) * diag

    r = lax.shift_right_logical(lax.broadcasted_iota(jnp.int32, (C_QK, C_WIDTH), 0), 5)
    c = lax.shift_right_logical(lax.broadcasted_iota(jnp.int32, (C_QK, C_WIDTH), 1), 6)
    expand = jnp.where(r == c, 1.0, 0.0).astype(BF16)
    gmat = _group_matrix(C_WIDTH, C_DV, 1.0 / C_DV, BF16)
    ii = lax.broadcasted_iota(jnp.int32, (sub, sub, C_QK), 0)
    jj = lax.broadcasted_iota(jnp.int32, (sub, sub, C_QK), 1)

    for sq, ci in [(sq, ci) for sq in range(seqs) for ci in range(block // sub)]:
        rows = slice(sq * block + ci * sub, sq * block + (ci + 1) * sub)
        q, k, v = q_ref[rows, :], k_ref[rows, :], v_ref[rows, :]
        b = _cumsum_rows(g_ref[rows, :])
        st = s_sc[sq]
        o_inter = lax.dot_general((q * jnp.exp(b)).astype(BF16), st.astype(BF16),
                                  (((1,), (1,)), ((), ())), preferred_element_type=F32)
        diff = b[:, None, :] - b[None, :, :]
        decay = jnp.exp(jnp.where(jj <= ii, diff, -jnp.inf))
        prod = decay * q[:, None, :] * k[None, :, :]
        att = jnp.dot(prod.reshape(sub * sub, C_QK).astype(BF16), expand, preferred_element_type=F32)
        o_intra = jnp.sum(att.reshape(sub, sub, C_WIDTH) * v[None, :, :], axis=1)
        o = o_inter + o_intra
        b_last = b[sub - 1:sub, :]
        kd = (k * jnp.exp(b_last - b)).astype(BF16)
        upd = lax.dot_general(v.astype(BF16), kd, (((0,), (0,)), ((), ())), preferred_element_type=F32)
        s_sc[sq] = jnp.exp(b_last) * st + upd * diag
        on = o * lax.rsqrt(_group_mean(o * o, gmat) + RMS_EPS) * gain_ref[...]
        o_ref[rows, :] = on * gate_ref[rows, :]

    @pl.when(tb == pl.num_programs(1) - 1)
    def _():
        for sq in range(seqs):
            st = s_sc[sq]
            s_t = st[0:C_DV]
            for h in range(1, C_HEADS):
                s_t = s_t + st[h * C_DV:(h + 1) * C_DV]
            s_out = jnp.concatenate([s_t, jnp.zeros((LANE - C_DV, C_QK), F32)], axis=0).T
            st_ref[sq] = s_out[:, :C_DV]


def _gla_call(cq, ck, cv, ga, gate, gain, s0, batch, seq, block, sub, seqs=1):
    nb = seq // block
    assert seqs == 1 or nb == 1
    tok = lambda w: pl.BlockSpec((seqs * block, w), lambda b, i: (b * nb + i, 0))
    st_spec = pl.BlockSpec((seqs, C_QK, C_DV), lambda b, i: (b, 0, 0))
    return pl.pallas_call(
        functools.partial(_gla_body, seqs=seqs, block=block, sub=sub),
        grid=(batch // seqs, nb),
        in_specs=[tok(C_QK), tok(C_QK), tok(C_WIDTH), tok(C_QK), tok(C_WIDTH),
                  _const_spec((1, C_WIDTH)), st_spec],
        out_specs=[tok(C_WIDTH), st_spec],
        out_shape=[jax.ShapeDtypeStruct((batch * seq, C_WIDTH), F32),
                   jax.ShapeDtypeStruct((batch, C_QK, C_DV), F32)],
        scratch_shapes=[pltpu.VMEM((seqs, C_WIDTH, C_QK), F32)],
        compiler_params=_cparams("parallel", "arbitrary"),
        name="gla",
    )(cq, ck, cv, ga, gate, gain, s0)


def _layer_weights(l, w):
    row = lambda v: v.reshape(1, -1).astype(F32)
    pad_cols = D_IN_PAD - D_IN
    blockdiag = lambda m: jax.scipy.linalg.block_diag(*[m[i] for i in range(B_BLOCKS)]).astype(BF16)
    return dict(
        ffn1=(row(w['ffn1_norm'][l]), w['ffn1_w_gate'][l].astype(BF16), w['ffn1_w_up'][l].astype(BF16),
              w['ffn1_w_down'][l].astype(BF16)),
        ffn2=(row(w['ffn2_norm'][l]), w['ffn2_w_gate'][l].astype(BF16), w['ffn2_w_up'][l].astype(BF16),
              w['ffn2_w_down'][l].astype(BF16)),
        mix_norm=row(w['mix_norm'][l]),
        w_in=jnp.pad(w['w_in'][l], ((0, 0), (0, pad_cols))).astype(BF16),
        w_out=w['w_out'][l].astype(BF16),
        gq=row(jnp.tile(w['a_q_norm'][l], MXU_WIDTH // A_QK_DIM)),
        gk=row(jnp.tile(w['a_k_norm'][l], MXU_WIDTH // A_QK_DIM)),
        lv=w['a_lambda'][l].astype(F32),
        go=row(w['a_out_norm'][l]),
        go_col=w['a_out_norm'][l].reshape(-1, 1).astype(F32),
        cw=w['b_conv_w'][l].astype(F32),
        cb=row(w['b_conv_b'][l]),
        wa=blockdiag(w['b_gate_a_w'][l]),
        wx=blockdiag(w['b_gate_x_w'][l]),
        ba=row(w['b_gate_a_b'][l]),
        bxb=row(w['b_gate_x_b'][l]),
        lam=row(w['b_lambda'][l]),
        w2=jnp.pad(w['c_alpha_w2'][l], ((0, LANE - C_RANK), (0, 0))).astype(BF16),
        ab=row(w['c_alpha_b'][l]),
        gc=row(jnp.tile(w['c_out_norm'][l], C_HEADS)),
    )


def _prompt_layer(x, p, lam_init, batch, seq):
    x1 = _ffn_call(x, *p['ffn1'])
    qt, k_t, kb, v, vt, bx, bg, cq, ck, cv, cg, ga = _mixin_call(
        x1, p['mix_norm'], p['w_in'], p['gq'], p['gk'], p['w2'], p['ab'], tiles_t=True, seq=seq)
    k = jnp.swapaxes(k_t, 1, 2)
    oa = _attn_prompt_call(qt, kb, vt, p['lv'], p['go_col'], lam_init, batch, seq)
    ob, tail, h_last = _rglru_prompt_call(
        bx, bg, jnp.zeros((batch, SUBLANE, B_WIDTH), F32), jnp.zeros((batch, 1, B_WIDTH), F32),
        p['cw'], p['cb'], p['wa'], p['wx'], p['ba'], p['bxb'], p['lam'], batch, seq)
    oc, st = _gla_call(cq, ck, cv, ga, cg, p['gc'], jnp.zeros((batch, C_QK, C_DV), F32),
                       batch, seq, GLA_BLOCK, GLA_SUB)
    x2 = _mixout_call(x1, oa, ob, oc, p['w_out'], *p['ffn2'])
    state = (k.reshape(batch, seq, A_HEADS, 2, A_QK_DIM), v.reshape(batch, seq, A_HEADS, A_V_DIM),
             tail[:, SUBLANE - (CONV_W - 1):, :], h_last[:, 0, :],
             st.reshape(batch, C_HEADS, C_DK, C_DV))
    return x2, state


def _sample_layer(x, p, lam_init, layer, batch, steps, page_table, cache_k, cache_v, conv0, h0, s0):
    x1 = _ffn_call(x, *p['ffn1'])
    qb, k, kb, v, vb, bx, bg, cq, ck, cv, cg, ga = _mixin_call(
        x1, p['mix_norm'], p['w_in'], p['gq'], p['gk'], p['w2'], p['ab'])

    seqd = lambda a: a.reshape(batch, steps, a.shape[-1])
    q3 = seqd(qb)
    pad_rows = lambda a: jnp.pad(a, ((0, 0), (0, SAMPLE_Q_ROWS - a.shape[1]), (0, 0)))
    qx = pad_rows(jnp.concatenate([q3, q3], axis=1))
    oa = _attn_sample_call(page_table, qx, pad_rows(seqd(kb)), pad_rows(seqd(vb)), p['lv'], p['go'],
                           cache_k, cache_v, layer, lam_init)
    oa = oa.reshape(batch * steps, A_WIDTH)

    tm = lambda a: jnp.swapaxes(seqd(a), 0, 1)
    ob_tm, cn_tm, h_last = _rglru_sample_call(
        tm(bx), tm(bg), jnp.swapaxes(conv0.astype(F32), 0, 1), h0.astype(F32),
        p['cw'], p['cb'], p['wa'], p['wx'], p['ba'], p['bxb'], p['lam'])
    ob = jnp.swapaxes(ob_tm, 0, 1).reshape(batch * steps, B_WIDTH)

    pad8 = lambda a: jnp.pad(seqd(a), ((0, 0), (0, SUBLANE - steps), (0, 0))).reshape(batch * SUBLANE, -1)
    oc_pad, st = _gla_call(pad8(cq), pad8(ck), pad8(cv), pad8(ga), pad8(cg), p['gc'],
                           s0.astype(F32).reshape(batch, C_QK, C_DV), batch, SUBLANE, SUBLANE, SUBLANE,
                           seqs=GLA_SAMPLE_SEQS)
    oc = oc_pad.reshape(batch, SUBLANE, C_WIDTH)[:, :steps].reshape(batch * steps, C_WIDTH)

    x2 = _mixout_call(x1, oa, ob, oc, p['w_out'], *p['ffn2'])
    state = (k.reshape(batch, steps, A_HEADS, 2, A_QK_DIM), v.reshape(batch, steps, A_HEADS, A_V_DIM),
             jnp.swapaxes(cn_tm, 0, 1), h_last, st.reshape(batch, C_HEADS, C_DK, C_DV))
    return x2, state


def kernel(x_prompt, x_sample, cache_k, cache_v, state_conv, state_rglru, state_gla, page_table,
           ffn1_norm, ffn1_w_gate, ffn1_w_up, ffn1_w_down, mix_norm, w_in, w_out,
           a_q_norm, a_k_norm, a_lambda, a_out_norm,
           b_conv_w, b_conv_b, b_gate_a_w, b_gate_a_b, b_gate_x_w, b_gate_x_b, b_lambda,
           c_alpha_w2, c_alpha_b, c_out_norm,
           ffn2_norm, ffn2_w_gate, ffn2_w_up, ffn2_w_down):
    weights = dict(
        ffn1_norm=ffn1_norm, ffn1_w_gate=ffn1_w_gate, ffn1_w_up=ffn1_w_up, ffn1_w_down=ffn1_w_down,
        mix_norm=mix_norm, w_in=w_in, w_out=w_out, a_q_norm=a_q_norm, a_k_norm=a_k_norm,
        a_lambda=a_lambda, a_out_norm=a_out_norm, b_conv_w=b_conv_w, b_conv_b=b_conv_b,
        b_gate_a_w=b_gate_a_w, b_gate_a_b=b_gate_a_b, b_gate_x_w=b_gate_x_w, b_gate_x_b=b_gate_x_b,
        b_lambda=b_lambda, c_alpha_w2=c_alpha_w2, c_alpha_b=c_alpha_b, c_out_norm=c_out_norm,
        ffn2_norm=ffn2_norm, ffn2_w_gate=ffn2_w_gate, ffn2_w_up=ffn2_w_up, ffn2_w_down=ffn2_w_down)
    bp, seq, _ = x_prompt.shape
    bs, steps, _ = x_sample.shape
    depth = ffn1_norm.shape[0]
    n_pool = cache_k.shape[1]
    ckt = jnp.transpose(cache_k, (0, 1, 3, 4, 5, 2)).reshape(depth, n_pool, A_WIDTH, PAGE_SIZE)
    cvh = cache_v.reshape(depth, n_pool, PAGE_SIZE * A_HEADS, A_V_DIM)

    yp = x_prompt.reshape(bp * seq, D_MODEL)
    ys = x_sample.reshape(bs * steps, D_MODEL)
    outs_p, outs_s = [], []
    for l in range(depth):
        lam_init = 0.8 - 0.6 * math.exp(-0.3 * l)
        p = _layer_weights(l, weights)
        yp, st_p = _prompt_layer(yp, p, lam_init, bp, seq)
        ys, st_s = _sample_layer(ys, p, lam_init, l, bs, steps, page_table, ckt, cvh,
                                 state_conv[l], state_rglru[l], state_gla[l])
        outs_p.append(st_p)
        outs_s.append(st_s)
    stack = lambda outs: [jnp.stack([o[i] for o in outs]) for i in range(5)]
    k_p, v_p, conv_p, h_p, s_p = stack(outs_p)
    k_s, v_s, conv_s, h_s, s_s = stack(outs_s)
    return (yp.reshape(bp, seq, D_MODEL), ys.reshape(bs, steps, D_MODEL),
            k_p, v_p, conv_p, h_p, s_p, k_s, v_s, conv_s, h_s, s_s)
```

```python
import functools
import math

import jax
import jax.numpy as jnp
from jax import lax
from jax.experimental import pallas as pl
from jax.experimental.pallas import tpu as pltpu

F32 = jnp.float32
BF16 = jnp.bfloat16

D_MODEL = 1024
DEPTH = 4
PAGE_SIZE = 128
A_HEADS = 4
A_V_DIM = 128
A_QK_DIM = 64
A_WIDTH = A_HEADS * A_V_DIM
B_WIDTH = 256
B_BLOCKS = 4
CONV_W = 4
RG_C = 8.0
C_WIDTH = 256
C_HEADS = 4
C_DV = 64
C_DK = 32
C_QK = C_HEADS * C_DK
C_RANK = 16
GLA_TAU = 16.0
D_FF = 2816
RMS_EPS = 1e-6

OFF_Q, OFF_K, OFF_V, OFF_BX, OFF_C = 0, 512, 1024, 1536, 2048
D_IN = 2832
D_IN_PAD = 2944

LANE = 128
SUBLANE = 8
MXU_WIDTH = 256
VMEM_LIMIT_BYTES = 56 * 1024 * 1024

TOKEN_TILE = 512
FF_CHUNK = D_FF // 2
ATT_TILE = 256
ACC_EXTRA_ROWS = 16
SCAN_TILE = 128
GLA_BLOCK = 256
GLA_SUB = 16
GLA_PROMPT_SEQS = 4
GLA_SAMPLE_SEQS = 8
SAMPLE_Q_ROWS = 16


def _cparams(*sem):
    return pltpu.CompilerParams(dimension_semantics=sem, vmem_limit_bytes=VMEM_LIMIT_BYTES)


def _const_spec(shape):
    zeros = (0,) * len(shape)
    return pl.BlockSpec(shape, lambda *_: zeros, pipeline_mode=pl.Buffered(1))


def _rms(x, gain):
    return x * lax.rsqrt(jnp.mean(x * x, axis=-1, keepdims=True) + RMS_EPS) * gain


def _softplus(y):
    return jnp.maximum(y, 0.0) + jnp.log1p(jnp.exp(-jnp.abs(y)))


def _gelu_tanh(x):
    return x * (0.5 * (1.0 + jnp.tanh(math.sqrt(2.0 / math.pi) * (x + 0.044715 * (x * x * x)))))


def _silu(x):
    return x * jax.nn.sigmoid(x)


def _group_matrix(n, group, value, dtype):
    shift = group.bit_length() - 1
    r = lax.shift_right_logical(lax.broadcasted_iota(jnp.int32, (n, n), 0), shift)
    c = lax.shift_right_logical(lax.broadcasted_iota(jnp.int32, (n, n), 1), shift)
    return jnp.where(r == c, value, 0.0).astype(dtype)


def _group_mean(sq, gmat):
    return jnp.dot(sq.astype(BF16), gmat, preferred_element_type=F32)


def _ffn_math(x, gain, wg_ref, wu_ref, wd_ref):
    hb = _rms(x, gain).astype(BF16)
    acc = x
    for c in range(D_FF // FF_CHUNK):
        sl = slice(c * FF_CHUNK, (c + 1) * FF_CHUNK)
        a = jnp.dot(hb, wg_ref[:, sl], preferred_element_type=F32)
        u = jnp.dot(hb, wu_ref[:, sl], preferred_element_type=F32)
        t = (_silu(a) * u).astype(BF16)
        acc = acc + 0.5 * jnp.dot(t, wd_ref[sl, :], preferred_element_type=F32)
    return acc


def _ffn_body(x_ref, g_ref, wg_ref, wu_ref, wd_ref, o_ref):
    o_ref[...] = _ffn_math(x_ref[...], g_ref[...], wg_ref, wu_ref, wd_ref)


def _ffn_call(x, gain, wg, wu, wd):
    n = x.shape[0]
    tm = TOKEN_TILE
    row = lambda i: (i, 0)
    return pl.pallas_call(
        _ffn_body,
        grid=(n // tm,),
        in_specs=[pl.BlockSpec((tm, D_MODEL), row), _const_spec((1, D_MODEL)),
                  _const_spec((D_MODEL, D_FF)), _const_spec((D_MODEL, D_FF)),
                  _const_spec((D_FF, D_MODEL))],
        out_specs=pl.BlockSpec((tm, D_MODEL), row),
        out_shape=jax.ShapeDtypeStruct((n, D_MODEL), F32),
        compiler_params=_cparams("parallel"),
        name="ffn",
    )(x, gain, wg, wu, wd)


def _mixin_body(x_ref, g_ref, win_ref, gq_ref, gk_ref, w2_ref, ab_ref,
                q_ref, k_ref, kb_ref, v_ref, vb_ref, bx_ref, bg_ref,
                cq_ref, ck_ref, cv_ref, cg_ref, ga_ref, *, tiles_t):
    hb = _rms(x_ref[...], g_ref[...]).astype(BF16)
    gmat = _group_matrix(MXU_WIDTH, A_QK_DIM, 1.0 / A_QK_DIM, BF16)
    t = ATT_TILE

    def qk_norm(z, gain):
        return z * lax.rsqrt(_group_mean(z * z, gmat) + RMS_EPS) * gain

    def put(ref, sl, val):
        if tiles_t:
            for u in range(val.shape[0] // t):
                ref[u, sl, :] = val[u * t:(u + 1) * t, :].T.astype(BF16)
        else:
            ref[:, sl] = val.astype(BF16)

    for c in range(A_WIDTH // MXU_WIDTH):
        sl = slice(c * MXU_WIDTH, (c + 1) * MXU_WIDTH)
        zq = jnp.dot(hb, win_ref[:, OFF_Q + c * MXU_WIDTH:OFF_Q + (c + 1) * MXU_WIDTH],
                     preferred_element_type=F32)
        put(q_ref, sl, qk_norm(zq, gq_ref[...]) * (A_QK_DIM ** -0.5))
        zk = jnp.dot(hb, win_ref[:, OFF_K + c * MXU_WIDTH:OFF_K + (c + 1) * MXU_WIDTH],
                     preferred_element_type=F32)
        kn = qk_norm(zk, gk_ref[...])
        if tiles_t:
            k_ref[sl, :] = kn.T
        else:
            k_ref[:, sl] = kn
        kb_ref[:, sl] = kn.astype(BF16)
        zv = jnp.dot(hb, win_ref[:, OFF_V + c * MXU_WIDTH:OFF_V + (c + 1) * MXU_WIDTH],
                     preferred_element_type=F32)
        if tiles_t:
            for hh in range(MXU_WIDTH // A_V_DIM):
                head = c * (MXU_WIDTH // A_V_DIM) + hh
                v_ref[pl.ds(head, zv.shape[0], stride=A_HEADS), :] = zv[:, hh * A_V_DIM:(hh + 1) * A_V_DIM]
        else:
            v_ref[:, sl] = zv
        put(vb_ref, sl, zv)
    zb = jnp.dot(hb, win_ref[:, OFF_BX:OFF_BX + 2 * B_WIDTH], preferred_element_type=F32)
    bx_ref[...] = zb[:, :B_WIDTH]
    bg_ref[...] = _gelu_tanh(zb[:, B_WIDTH:])
    zc = jnp.dot(hb, win_ref[:, OFF_C:D_IN_PAD], preferred_element_type=F32)
    cq_ref[...] = zc[:, 0:C_QK] * (C_DK ** -0.5)
    ck_ref[...] = zc[:, C_QK:2 * C_QK]
    cv_ref[...] = zc[:, 2 * C_QK:2 * C_QK + C_WIDTH]
    cg_ref[...] = _silu(zc[:, 2 * C_QK + C_WIDTH:2 * C_QK + 2 * C_WIDTH])
    za = zc[:, 2 * C_QK + 2 * C_WIDTH:].astype(BF16)
    pre = jnp.dot(za, w2_ref[...], preferred_element_type=F32) + ab_ref[...]
    ga_ref[...] = -_softplus(-pre) * (1.0 / GLA_TAU)


def _mixin_call(x, gain, w_in, gq, gk, w2, ab, tiles_t=False, seq=None):
    n = x.shape[0]
    tm = TOKEN_TILE
    t = ATT_TILE
    row = lambda i: (i, 0)
    widths = [None, (A_WIDTH, F32), (A_WIDTH, BF16), (A_WIDTH, F32), None,
              (B_WIDTH, F32), (B_WIDTH, F32),
              (C_QK, F32), (C_QK, F32), (C_WIDTH, F32), (C_WIDTH, F32), (C_QK, F32)]
    if tiles_t:
        qv_spec = pl.BlockSpec((tm // t, A_WIDTH, t), lambda i: (i, 0, 0))
        qv_shape = jax.ShapeDtypeStruct((n // t, A_WIDTH, t), BF16)
        per_seq = seq // tm
        k_spec = pl.BlockSpec((None, A_WIDTH, tm), lambda i: (i // per_seq, 0, i % per_seq))
        k_shape = jax.ShapeDtypeStruct((n // seq, A_WIDTH, seq), F32)
        v_spec = pl.BlockSpec((tm * A_HEADS, A_V_DIM), row)
        v_shape = jax.ShapeDtypeStruct((n * A_HEADS, A_V_DIM), F32)
    else:
        v_spec = pl.BlockSpec((tm, A_WIDTH), row)
        v_shape = jax.ShapeDtypeStruct((n, A_WIDTH), F32)
        qv_spec = pl.BlockSpec((tm, A_WIDTH), row)
        qv_shape = jax.ShapeDtypeStruct((n, A_WIDTH), BF16)
        k_spec = pl.BlockSpec((tm, A_WIDTH), row)
        k_shape = jax.ShapeDtypeStruct((n, A_WIDTH), F32)
    return pl.pallas_call(
        functools.partial(_mixin_body, tiles_t=tiles_t),
        grid=(n // tm,),
        in_specs=[pl.BlockSpec((tm, D_MODEL), row), _const_spec((1, D_MODEL)),
                  _const_spec((D_MODEL, D_IN_PAD)), _const_spec((1, MXU_WIDTH)),
                  _const_spec((1, MXU_WIDTH)), _const_spec((LANE, C_QK)), _const_spec((1, C_QK))],
        out_specs=[qv_spec if w is None else k_spec if i == 1 else v_spec if i == 3
                   else pl.BlockSpec((tm, w[0]), row) for i, w in enumerate(widths)],
        out_shape=[qv_shape if w is None else k_shape if i == 1 else v_shape if i == 3
                   else jax.ShapeDtypeStruct((n, w[0]), w[1]) for i, w in enumerate(widths)],
        compiler_params=_cparams("parallel"),
        name="mixin",
    )(x, gain, w_in, gq, gk, w2, ab)


def _mixout_body(x_ref, oa_ref, ob_ref, oc_ref, wo_ref, g_ref, wg_ref, wu_ref, wd_ref, o_ref):
    mix = jnp.dot(oa_ref[...].astype(BF16), wo_ref[0:A_WIDTH, :], preferred_element_type=F32)
    mix = mix + jnp.dot(ob_ref[...].astype(BF16), wo_ref[A_WIDTH:A_WIDTH + B_WIDTH, :],
                        preferred_element_type=F32)
    mix = mix + jnp.dot(oc_ref[...].astype(BF16), wo_ref[A_WIDTH + B_WIDTH:, :],
                        preferred_element_type=F32)
    o_ref[...] = _ffn_math(x_ref[...] + mix, g_ref[...], wg_ref, wu_ref, wd_ref)


def _mixout_call(x, oa, ob, oc, w_out, gain, wg, wu, wd):
    n = x.shape[0]
    tm = TOKEN_TILE
    row = lambda i: (i, 0)
    return pl.pallas_call(
        _mixout_body,
        grid=(n // tm,),
        in_specs=[pl.BlockSpec((tm, D_MODEL), row), pl.BlockSpec((tm, A_WIDTH), row),
                  pl.BlockSpec((tm, B_WIDTH), row), pl.BlockSpec((tm, C_WIDTH), row),
                  _const_spec((D_MODEL, D_MODEL)), _const_spec((1, D_MODEL)),
                  _const_spec((D_MODEL, D_FF)), _const_spec((D_MODEL, D_FF)),
                  _const_spec((D_FF, D_MODEL))],
        out_specs=pl.BlockSpec((tm, D_MODEL), row),
        out_shape=jax.ShapeDtypeStruct((n, D_MODEL), F32),
        compiler_params=_cparams("parallel"),
        name="mixout_ffn",
    )(x, oa, ob, oc, w_out, gain, wg, wu, wd)


def _alibi_slope(h):
    return 2.0 ** (-8.0 * (h + 1) / A_HEADS)


def _diff_lambda(lv_ref, lam_init):
    lv = lv_ref[...]
    s01 = jnp.sum(lv[0:1, :] * lv[1:2, :], axis=1, keepdims=True)
    s23 = jnp.sum(lv[2:3, :] * lv[3:4, :], axis=1, keepdims=True)
    return jnp.exp(s01) - jnp.exp(s23) + lam_init


def _attn_prompt_body(qt_ref, k_ref, vt_ref, lv_ref, go_ref, o_ref,
                      qx_sc, sa_sc, sb_sc, m_sc, acc_sc, *, lam_init):
    t = ATT_TILE
    qi = pl.program_id(1)
    lam = _diff_lambda(lv_ref, lam_init)
    sub = lax.broadcasted_iota(jnp.int32, (LANE, t), 0)
    krow = lax.broadcasted_iota(jnp.int32, (t, 2 * t), 0)
    qcol = lax.broadcasted_iota(jnp.int32, (t, 2 * t), 1)
    causal = krow <= jnp.where(qcol >= t, qcol - t, qcol)

    heads = [(h, slice(h * LANE, (h + 1) * LANE), _alibi_slope(h)) for h in range(A_HEADS)]

    arow = lax.broadcasted_iota(jnp.int32, (LANE, 2 * t), 0)
    klane = lax.broadcasted_iota(jnp.int32, (t, LANE), 1)
    koff = lax.broadcasted_iota(jnp.int32, (t, LANE), 0).astype(F32)
    key_offset = jnp.where(klane == 0, koff, 0.0).astype(BF16)
    ones_rows = jnp.ones((ACC_EXTRA_ROWS, t), BF16)

    for h, hs, slope in heads:
        qt = qt_ref[hs, :]
        zero = jnp.zeros_like(qt)
        slope_rows = jnp.where(arow == 0, slope, 0.0).astype(BF16)
        qx_sc[h] = jnp.concatenate(
            [jnp.concatenate([jnp.where(sub < A_QK_DIM, qt, zero),
                              jnp.where(sub >= A_QK_DIM, qt, zero)], axis=1), slope_rows], axis=0)
        m_sc[h] = jnp.full((1, 2 * t), -jnp.inf, F32)
        acc_sc[h] = jnp.zeros((LANE + ACC_EXTRA_ROWS, 2 * t), F32)

    def scores(kj, s_ref):
        k0 = pl.multiple_of(kj * t, t)
        for h, hs, _ in heads:
            ka = jnp.concatenate([k_ref[pl.ds(k0, t), hs], key_offset], axis=1)
            s_ref[h] = jnp.dot(ka, qx_sc[h], preferred_element_type=F32)

    def update(kj, s_ref, masked):
        dist = ((kj - qi) * t).astype(F32)
        for h, hs, slope in heads:
            va = jnp.concatenate([vt_ref[kj, hs, :], ones_rows], axis=0)
            s = s_ref[h]
            if masked:
                s = jnp.where(causal, s, -jnp.inf)
            c = slope * dist
            m_old = m_sc[h]
            m_new = jnp.maximum(m_old, jnp.max(s, axis=0, keepdims=True) + c)
            alpha = jnp.exp(m_old - m_new)
            p = jnp.exp(s - (m_new - c))
            acc_sc[h] = alpha * acc_sc[h] + jnp.dot(va, p.astype(BF16), preferred_element_type=F32)
            m_sc[h] = m_new

    def pair(i, carry):
        kj = 2 * i
        scores(kj + 1, sb_sc)
        update(kj, sa_sc, False)
        scores(kj + 2, sa_sc)
        update(kj + 1, sb_sc, False)
        return carry

    scores(0, sa_sc)
    lax.fori_loop(0, qi // 2, pair, 0)
    odd = qi % 2 == 1

    @pl.when(odd)
    def _():
        scores(qi, sb_sc)
        update(qi - 1, sa_sc, False)
        update(qi, sb_sc, True)

    @pl.when(jnp.logical_not(odd))
    def _():
        update(qi, sa_sc, True)

    for h, hs, _ in heads:
        acc = acc_sc[h]
        o = acc[:LANE] / acc[LANE:LANE + 1]
        od = o[:, :t] - lam * o[:, t:]
        ms = jnp.mean(od * od, axis=0, keepdims=True)
        on = od * lax.rsqrt(ms + RMS_EPS) * go_ref[...] * (1.0 - lam_init)
        o_ref[:, hs] = on.T


def _attn_prompt_call(qt, kb, vt, lv, go_col, lam_init, batch, seq):
    t = ATT_TILE
    nt = seq // t
    return pl.pallas_call(
        functools.partial(_attn_prompt_body, lam_init=lam_init),
        grid=(batch, nt),
        in_specs=[pl.BlockSpec((None, A_WIDTH, t), lambda b, i: (b * nt + i, 0, 0)),
                  pl.BlockSpec((seq, A_WIDTH), lambda b, i: (b, 0)),
                  pl.BlockSpec((nt, A_WIDTH, t), lambda b, i: (b, 0, 0)),
                  _const_spec((4, A_QK_DIM)), _const_spec((A_V_DIM, 1))],
        out_specs=pl.BlockSpec((t, A_WIDTH), lambda b, i: (b * nt + i, 0)),
        out_shape=jax.ShapeDtypeStruct((batch * seq, A_WIDTH), F32),
        scratch_shapes=[pltpu.VMEM((A_HEADS, 2 * LANE, 2 * t), BF16),
                        pltpu.VMEM((A_HEADS, t, 2 * t), F32), pltpu.VMEM((A_HEADS, t, 2 * t), F32),
                        pltpu.VMEM((A_HEADS, 1, 2 * t), F32),
                        pltpu.VMEM((A_HEADS, LANE + ACC_EXTRA_ROWS, 2 * t), F32)],
        compiler_params=_cparams("parallel", "arbitrary"),
        name="attn_prompt",
    )(qt, kb, vt, lv, go_col)


def _attn_sample_body(pt_ref, q_ref, kn_ref, vn_ref, lv_ref, go_ref, *rest, lam_init, n_pages, n_new):
    del pt_ref
    k_pages = rest[:n_pages]
    v_pages = rest[n_pages:2 * n_pages]
    o_ref = rest[2 * n_pages]
    rows = SAMPLE_Q_ROWS
    past = n_pages * PAGE_SIZE
    lam = _diff_lambda(lv_ref, lam_init)
    row = lax.broadcasted_iota(jnp.int32, (rows, 1), 0)
    tok = jnp.bitwise_and(row, n_new - 1)
    qpos = past + tok
    lane = lax.broadcasted_iota(jnp.int32, (rows, LANE), 1)
    is_map1 = row < n_new
    is_map2 = jnp.logical_and(row >= n_new, row < 2 * n_new)
    kpos_past = lax.broadcasted_iota(jnp.int32, (1, past), 1)
    new_col = lax.broadcasted_iota(jnp.int32, (1, rows), 1)
    kpos_new = past + new_col

    for h in range(A_HEADS):
        hs = slice(h * LANE, (h + 1) * LANE)
        slope = _alibi_slope(h)
        qh = q_ref[:, hs].astype(F32)
        qx = jnp.where(lane < A_QK_DIM, jnp.where(is_map1, qh, 0.0), jnp.where(is_map2, qh, 0.0))
        qx = qx.astype(BF16)
        nt = (((1,), (1,)), ((), ()))
        s_parts = [jnp.dot(qx, k_pages[p][hs, :].astype(BF16), preferred_element_type=F32)
                   for p in range(n_pages)]
        s = jnp.concatenate(s_parts, axis=1)
        s = s - slope * (qpos - kpos_past).astype(F32)
        sn = lax.dot_general(qx, kn_ref[:, hs], nt, preferred_element_type=F32)
        sn = sn - slope * (qpos - kpos_new).astype(F32)
        sn = jnp.where(new_col <= tok, sn, -jnp.inf)
        m = jnp.maximum(jnp.max(s, axis=1, keepdims=True), jnp.max(sn, axis=1, keepdims=True))
        p = jnp.exp(s - m)
        pn = jnp.exp(sn - m)
        l = jnp.sum(p, axis=1, keepdims=True) + jnp.sum(pn, axis=1, keepdims=True)
        pb = p.astype(BF16)
        acc = jnp.dot(pn.astype(BF16), vn_ref[:, hs], preferred_element_type=F32)
        for pg in range(n_pages):
            acc = acc + jnp.dot(pb[:, pg * PAGE_SIZE:(pg + 1) * PAGE_SIZE],
                                v_pages[pg][pl.ds(h, PAGE_SIZE, stride=A_HEADS), :].astype(BF16),
                                preferred_element_type=F32)
        o = acc / l
        od = o[0:n_new] - lam * o[n_new:2 * n_new]
        o_ref[:, hs] = _rms(od, go_ref[...]) * (1.0 - lam_init)


def _attn_sample_call(page_table, qx, kn, vn, lv, go, cache_kt, cache_vr, layer, lam_init):
    batch, n_pages = page_table.shape
    n_new = 4
    rows = SAMPLE_Q_ROWS
    tok_spec = pl.BlockSpec((None, rows, A_WIDTH), lambda b, pt: (b, 0, 0))

    def k_spec(p):
        return pl.BlockSpec((None, None, A_WIDTH, PAGE_SIZE), lambda b, pt: (layer, pt[b, p], 0, 0))

    def v_spec(p):
        return pl.BlockSpec((None, None, PAGE_SIZE * A_HEADS, A_V_DIM),
                            lambda b, pt: (layer, pt[b, p], 0, 0))

    const = lambda shape: pl.BlockSpec(shape, lambda b, pt: (0, 0))
    grid_spec = pltpu.PrefetchScalarGridSpec(
        num_scalar_prefetch=1,
        grid=(batch,),
        in_specs=([tok_spec, tok_spec, tok_spec, const((4, A_QK_DIM)), const((1, A_V_DIM))]
                  + [k_spec(p) for p in range(n_pages)]
                  + [v_spec(p) for p in range(n_pages)]),
        out_specs=pl.BlockSpec((None, n_new, A_WIDTH), lambda b, pt: (b, 0, 0)),
    )
    return pl.pallas_call(
        functools.partial(_attn_sample_body, lam_init=lam_init, n_pages=n_pages, n_new=n_new),
        grid_spec=grid_spec,
        out_shape=jax.ShapeDtypeStruct((batch, n_new, A_WIDTH), F32),
        compiler_params=_cparams("parallel"),
        name="attn_sample",
    )(page_table, qx, kn, vn, lv, go, *([cache_kt] * n_pages), *([cache_vr] * n_pages))


def _expm1(x):
    u = jnp.exp(x)
    return jnp.where(u == 1.0, x, (u - 1.0) * x / jnp.log(u))


def _rglru_gates(xc, wa_ref, wx_ref, ba, bx, lam):
    xb = xc.astype(BF16)
    r = jax.nn.sigmoid(jnp.dot(xb, wa_ref[...], preferred_element_type=F32) + ba)
    i = jax.nn.sigmoid(jnp.dot(xb, wx_ref[...], preferred_element_type=F32) + bx)
    log_a = (-RG_C * _softplus(-lam)) * r
    a = jnp.exp(log_a)
    u = jnp.sqrt(-_expm1(2.0 * log_a)) * (i * xc)
    return a, u


def _scan_rows(a, b):
    n = a.shape[0]
    row = lax.broadcasted_iota(jnp.int32, a.shape, 0)
    s = 1
    while s < n:
        keep = row >= s
        b = jnp.where(keep, a * pltpu.roll(b, s, axis=0) + b, b)
        a = jnp.where(keep, a * pltpu.roll(a, s, axis=0), a)
        s *= 2
    return a, b


def _rglru_prompt_body(x_ref, gate_ref, c0_ref, h0_ref, cw_ref, cb_ref, wa_ref, wx_ref,
                       ba_ref, bx_ref, lam_ref, o_ref, tail_ref, hl_ref, *, seq):
    t = SCAN_TILE
    cw = cw_ref[...]
    cb, ba, bx, lam = cb_ref[...], ba_ref[...], bx_ref[...], lam_ref[...]

    def block(blk, h):
        t0 = pl.multiple_of(blk * t, t)
        xb = x_ref[pl.ds(t0, t), :]
        tp = pl.multiple_of(jnp.maximum(t0 - SUBLANE, 0), SUBLANE)
        prev = jnp.where(blk == 0, c0_ref[...], x_ref[pl.ds(tp, SUBLANE), :])
        xx = jnp.concatenate([prev, xb], axis=0)
        xc = cb
        for j in range(CONV_W):
            d = CONV_W - 1 - j
            sh = xb if d == 0 else pltpu.roll(xx, d, axis=0)[SUBLANE:SUBLANE + t]
            xc = xc + cw[j:j + 1, :] * sh
        a, u = _rglru_gates(xc, wa_ref, wx_ref, ba, bx, lam)
        pa, hb = _scan_rows(a, u)
        hs = hb + pa * h
        o_ref[pl.ds(t0, t), :] = hs * gate_ref[pl.ds(t0, t), :]
        return hs[t - 1:t, :]

    h_last = lax.fori_loop(0, seq // t, block, h0_ref[...])
    hl_ref[...] = h_last
    tail_ref[...] = x_ref[seq - SUBLANE:seq, :]


def _rglru_prompt_call(bx, gate, c0, h0, cw, cb, wa, wx, ba, bxb, lam, batch, seq):
    seq_spec = pl.BlockSpec((seq, B_WIDTH), lambda b: (b, 0))
    vec = _const_spec((1, B_WIDTH))
    return pl.pallas_call(
        functools.partial(_rglru_prompt_body, seq=seq),
        grid=(batch,),
        in_specs=[seq_spec, seq_spec,
                  pl.BlockSpec((None, SUBLANE, B_WIDTH), lambda b: (b, 0, 0)),
                  pl.BlockSpec((None, 1, B_WIDTH), lambda b: (b, 0, 0)),
                  _const_spec((CONV_W, B_WIDTH)), vec,
                  _const_spec((B_WIDTH, B_WIDTH)), _const_spec((B_WIDTH, B_WIDTH)), vec, vec, vec],
        out_specs=[seq_spec,
                   pl.BlockSpec((None, SUBLANE, B_WIDTH), lambda b: (b, 0, 0)),
                   pl.BlockSpec((None, 1, B_WIDTH), lambda b: (b, 0, 0))],
        out_shape=[jax.ShapeDtypeStruct((batch * seq, B_WIDTH), F32),
                   jax.ShapeDtypeStruct((batch, SUBLANE, B_WIDTH), F32),
                   jax.ShapeDtypeStruct((batch, 1, B_WIDTH), F32)],
        compiler_params=_cparams("parallel"),
        name="rglru_prompt",
    )(bx, gate, c0, h0, cw, cb, wa, wx, ba, bxb, lam)


def _rglru_sample_body(x_ref, gate_ref, c0_ref, h0_ref, cw_ref, cb_ref, wa_ref, wx_ref,
                       ba_ref, bx_ref, lam_ref, o_ref, cn_ref, hl_ref, *, steps):
    cw = cw_ref[...]
    cb, ba, bx, lam = cb_ref[...], ba_ref[...], bx_ref[...], lam_ref[...]
    xs = [c0_ref[j] for j in range(CONV_W - 1)] + [x_ref[s] for s in range(steps)]
    h = h0_ref[...]
    for s in range(steps):
        xc = cb
        for j in range(CONV_W):
            xc = xc + cw[j:j + 1, :] * xs[s + j]
        a, u = _rglru_gates(xc, wa_ref, wx_ref, ba, bx, lam)
        h = a * h + u
        o_ref[s] = h * gate_ref[s]
    for j in range(CONV_W - 1):
        cn_ref[j] = xs[steps + j]
    hl_ref[...] = h


def _rglru_sample_call(x_tm, gate_tm, c0_tm, h0, cw, cb, wa, wx, ba, bxb, lam):
    steps, batch, _ = x_tm.shape
    return pl.pallas_call(
        functools.partial(_rglru_sample_body, steps=steps),
        out_shape=[jax.ShapeDtypeStruct((steps, batch, B_WIDTH), F32),
                   jax.ShapeDtypeStruct((CONV_W - 1, batch, B_WIDTH), F32),
                   jax.ShapeDtypeStruct((batch, B_WIDTH), F32)],
        name="rglru_sample",
    )(x_tm, gate_tm, c0_tm, h0, cw, cb, wa, wx, ba, bxb, lam)


def _cumsum_rows(x):
    n = x.shape[0]
    row = lax.broadcasted_iota(jnp.int32, x.shape, 0)
    s = 1
    while s < n:
        x = x + jnp.where(row >= s, pltpu.roll(x, s, axis=0), 0.0)
        s *= 2
    return x


def _gla_body(q_ref, k_ref, v_ref, g_ref, gate_ref, gain_ref, s0_ref, o_ref, st_ref, s_sc,
              *, seqs, block, sub):
    tb = pl.program_id(1)
    rt = lax.shift_right_logical(lax.broadcasted_iota(jnp.int32, (C_WIDTH, C_QK), 0), 6)
    ct = lax.shift_right_logical(lax.broadcasted_iota(jnp.int32, (C_WIDTH, C_QK), 1), 5)
    diag = jnp.where(rt == ct, 1.0, 0.0)

    @pl.when(tb == 0)
    def _():
        for sq in range(seqs):
            s_in = jnp.concatenate([s0_ref[sq], jnp.zeros((C_QK, LANE - C_DV), F32)], axis=1)
            s_t = s_in.T[:C_DV, :]
            s_sc[sq] = jnp.concatenate([s_t] * C_HEADS, axis=0) * diag

    r = lax.shift_right_logical(lax.broadcasted_iota(jnp.int32, (C_QK, C_WIDTH), 0), 5)
    c = lax.shift_right_logical(lax.broadcasted_iota(jnp.int32, (C_QK, C_WIDTH), 1), 6)
    expand = jnp.where(r == c, 1.0, 0.0).astype(BF16)
    gmat = _group_matrix(C_WIDTH, C_DV, 1.0 / C_DV, BF16)
    ii = lax.broadcasted_iota(jnp.int32, (sub, sub, C_QK), 0)
    jj = lax.broadcasted_iota(jnp.int32, (sub, sub, C_QK), 1)

    for ci, sq in [(ci, sq) for ci in range(block // sub) for sq in range(seqs)]:
        rows = slice(ci * sub, (ci + 1) * sub)
        q, k, v = q_ref[sq, rows, :], k_ref[sq, rows, :], v_ref[sq, rows, :]
        b = _cumsum_rows(g_ref[sq, rows, :])
        st = s_sc[sq]
        o_inter = lax.dot_general((q * jnp.exp(b)).astype(BF16), st.astype(BF16),
                                  (((1,), (1,)), ((), ())), preferred_element_type=F32)
        diff = b[:, None, :] - b[None, :, :]
        decay = jnp.exp(jnp.where(jj <= ii, diff, -jnp.inf))
        prod = decay * q[:, None, :] * k[None, :, :]
        att = jnp.dot(prod.reshape(sub * sub, C_QK).astype(BF16), expand, preferred_element_type=F32)
        o_intra = jnp.sum(att.reshape(sub, sub, C_WIDTH) * v[None, :, :], axis=1)
        o = o_inter + o_intra
        b_last = b[sub - 1:sub, :]
        kd = (k * jnp.exp(b_last - b)).astype(BF16)
        upd = lax.dot_general(v.astype(BF16), kd, (((0,), (0,)), ((), ())), preferred_element_type=F32)
        s_sc[sq] = jnp.exp(b_last) * st + upd * diag
        on = o * lax.rsqrt(_group_mean(o * o, gmat) + RMS_EPS) * gain_ref[...]
        o_ref[sq, rows, :] = on * gate_ref[sq, rows, :]

    @pl.when(tb == pl.num_programs(1) - 1)
    def _():
        for sq in range(seqs):
            st = s_sc[sq]
            s_t = st[0:C_DV]
            for h in range(1, C_HEADS):
                s_t = s_t + st[h * C_DV:(h + 1) * C_DV]
            s_out = jnp.concatenate([s_t, jnp.zeros((LANE - C_DV, C_QK), F32)], axis=0).T
            st_ref[sq] = s_out[:, :C_DV]


def _gla_call(cq, ck, cv, ga, gate, gain, s0, batch, seq, block, sub, seqs=1):
    nb = seq // block
    tok = lambda w: pl.BlockSpec((seqs, block, w), lambda b, i: (b, i, 0))
    st_spec = pl.BlockSpec((seqs, C_QK, C_DV), lambda b, i: (b, 0, 0))
    seq3 = lambda a: a.reshape(batch, seq, a.shape[-1])
    o, st = pl.pallas_call(
        functools.partial(_gla_body, seqs=seqs, block=block, sub=sub),
        grid=(batch // seqs, nb),
        in_specs=[tok(C_QK), tok(C_QK), tok(C_WIDTH), tok(C_QK), tok(C_WIDTH),
                  _const_spec((1, C_WIDTH)), st_spec],
        out_specs=[tok(C_WIDTH), st_spec],
        out_shape=[jax.ShapeDtypeStruct((batch, seq, C_WIDTH), F32),
                   jax.ShapeDtypeStruct((batch, C_QK, C_DV), F32)],
        scratch_shapes=[pltpu.VMEM((seqs, C_WIDTH, C_QK), F32)],
        compiler_params=_cparams("parallel", "arbitrary"),
        name="gla",
    )(seq3(cq), seq3(ck), seq3(cv), seq3(ga), seq3(gate), gain, s0)
    return o.reshape(batch * seq, C_WIDTH), st


def _layer_weights(l, w):
    row = lambda v: v.reshape(1, -1).astype(F32)
    pad_cols = D_IN_PAD - D_IN
    blockdiag = lambda m: jax.scipy.linalg.block_diag(*[m[i] for i in range(B_BLOCKS)]).astype(BF16)
    return dict(
        ffn1=(row(w['ffn1_norm'][l]), w['ffn1_w_gate'][l].astype(BF16), w['ffn1_w_up'][l].astype(BF16),
              w['ffn1_w_down'][l].astype(BF16)),
        ffn2=(row(w['ffn2_norm'][l]), w['ffn2_w_gate'][l].astype(BF16), w['ffn2_w_up'][l].astype(BF16),
              w['ffn2_w_down'][l].astype(BF16)),
        mix_norm=row(w['mix_norm'][l]),
        w_in=jnp.pad(w['w_in'][l], ((0, 0), (0, pad_cols))).astype(BF16),
        w_out=w['w_out'][l].astype(BF16),
        gq=row(jnp.tile(w['a_q_norm'][l], MXU_WIDTH // A_QK_DIM)),
        gk=row(jnp.tile(w['a_k_norm'][l], MXU_WIDTH // A_QK_DIM)),
        lv=w['a_lambda'][l].astype(F32),
        go=row(w['a_out_norm'][l]),
        go_col=w['a_out_norm'][l].reshape(-1, 1).astype(F32),
        cw=w['b_conv_w'][l].astype(F32),
        cb=row(w['b_conv_b'][l]),
        wa=blockdiag(w['b_gate_a_w'][l]),
        wx=blockdiag(w['b_gate_x_w'][l]),
        ba=row(w['b_gate_a_b'][l]),
        bxb=row(w['b_gate_x_b'][l]),
        lam=row(w['b_lambda'][l]),
        w2=jnp.pad(w['c_alpha_w2'][l], ((0, LANE - C_RANK), (0, 0))).astype(BF16),
        ab=row(w['c_alpha_b'][l]),
        gc=row(jnp.tile(w['c_out_norm'][l], C_HEADS)),
    )


def _prompt_layer(x, p, lam_init, batch, seq):
    x1 = _ffn_call(x, *p['ffn1'])
    qt, k_t, kb, v, vt, bx, bg, cq, ck, cv, cg, ga = _mixin_call(
        x1, p['mix_norm'], p['w_in'], p['gq'], p['gk'], p['w2'], p['ab'], tiles_t=True, seq=seq)
    k = jnp.swapaxes(k_t, 1, 2)
    oa = _attn_prompt_call(qt, kb, vt, p['lv'], p['go_col'], lam_init, batch, seq)
    ob, tail, h_last = _rglru_prompt_call(
        bx, bg, jnp.zeros((batch, SUBLANE, B_WIDTH), F32), jnp.zeros((batch, 1, B_WIDTH), F32),
        p['cw'], p['cb'], p['wa'], p['wx'], p['ba'], p['bxb'], p['lam'], batch, seq)
    oc, st = _gla_call(cq, ck, cv, ga, cg, p['gc'], jnp.zeros((batch, C_QK, C_DV), F32),
                       batch, seq, GLA_BLOCK, GLA_SUB, seqs=math.gcd(batch, GLA_PROMPT_SEQS))
    x2 = _mixout_call(x1, oa, ob, oc, p['w_out'], *p['ffn2'])
    state = (k.reshape(batch, seq, A_HEADS, 2, A_QK_DIM), v.reshape(batch, seq, A_HEADS, A_V_DIM),
             tail[:, SUBLANE - (CONV_W - 1):, :], h_last[:, 0, :],
             st.reshape(batch, C_HEADS, C_DK, C_DV))
    return x2, state


def _sample_layer(x, p, lam_init, layer, batch, steps, page_table, cache_k, cache_v, conv0, h0, s0):
    x1 = _ffn_call(x, *p['ffn1'])
    qb, k, kb, v, vb, bx, bg, cq, ck, cv, cg, ga = _mixin_call(
        x1, p['mix_norm'], p['w_in'], p['gq'], p['gk'], p['w2'], p['ab'])

    seqd = lambda a: a.reshape(batch, steps, a.shape[-1])
    q3 = seqd(qb)
    pad_rows = lambda a: jnp.pad(a, ((0, 0), (0, SAMPLE_Q_ROWS - a.shape[1]), (0, 0)))
    qx = pad_rows(jnp.concatenate([q3, q3], axis=1))
    oa = _attn_sample_call(page_table, qx, pad_rows(seqd(kb)), pad_rows(seqd(vb)), p['lv'], p['go'],
                           cache_k, cache_v, layer, lam_init)
    oa = oa.reshape(batch * steps, A_WIDTH)

    tm = lambda a: jnp.swapaxes(seqd(a), 0, 1)
    ob_tm, cn_tm, h_last = _rglru_sample_call(
        tm(bx), tm(bg), jnp.swapaxes(conv0.astype(F32), 0, 1), h0.astype(F32),
        p['cw'], p['cb'], p['wa'], p['wx'], p['ba'], p['bxb'], p['lam'])
    ob = jnp.swapaxes(ob_tm, 0, 1).reshape(batch * steps, B_WIDTH)

    pad8 = lambda a: jnp.pad(seqd(a), ((0, 0), (0, SUBLANE - steps), (0, 0))).reshape(batch * SUBLANE, -1)
    oc_pad, st = _gla_call(pad8(cq), pad8(ck), pad8(cv), pad8(ga), pad8(cg), p['gc'],
                           s0.astype(F32).reshape(batch, C_QK, C_DV), batch, SUBLANE, SUBLANE, SUBLANE,
                           seqs=GLA_SAMPLE_SEQS)
    oc = oc_pad.reshape(batch, SUBLANE, C_WIDTH)[:, :steps].reshape(batch * steps, C_WIDTH)

    x2 = _mixout_call(x1, oa, ob, oc, p['w_out'], *p['ffn2'])
    state = (k.reshape(batch, steps, A_HEADS, 2, A_QK_DIM), v.reshape(batch, steps, A_HEADS, A_V_DIM),
             jnp.swapaxes(cn_tm, 0, 1), h_last, st.reshape(batch, C_HEADS, C_DK, C_DV))
    return x2, state


def kernel(x_prompt, x_sample, cache_k, cache_v, state_conv, state_rglru, state_gla, page_table,
           ffn1_norm, ffn1_w_gate, ffn1_w_up, ffn1_w_down, mix_norm, w_in, w_out,
           a_q_norm, a_k_norm, a_lambda, a_out_norm,
           b_conv_w, b_conv_b, b_gate_a_w, b_gate_a_b, b_gate_x_w, b_gate_x_b, b_lambda,
           c_alpha_w2, c_alpha_b, c_out_norm,
           ffn2_norm, ffn2_w_gate, ffn2_w_up, ffn2_w_down):
    weights = dict(
        ffn1_norm=ffn1_norm, ffn1_w_gate=ffn1_w_gate, ffn1_w_up=ffn1_w_up, ffn1_w_down=ffn1_w_down,
        mix_norm=mix_norm, w_in=w_in, w_out=w_out, a_q_norm=a_q_norm, a_k_norm=a_k_norm,
        a_lambda=a_lambda, a_out_norm=a_out_norm, b_conv_w=b_conv_w, b_conv_b=b_conv_b,
        b_gate_a_w=b_gate_a_w, b_gate_a_b=b_gate_a_b, b_gate_x_w=b_gate_x_w, b_gate_x_b=b_gate_x_b,
        b_lambda=b_lambda, c_alpha_w2=c_alpha_w2, c_alpha_b=c_alpha_b, c_out_norm=c_out_norm,
        ffn2_norm=ffn2_norm, ffn2_w_gate=ffn2_w_gate, ffn2_w_up=ffn2_w_up, ffn2_w_down=ffn2_w_down)
    bp, seq, _ = x_prompt.shape
    bs, steps, _ = x_sample.shape
    depth = ffn1_norm.shape[0]
    n_pool = cache_k.shape[1]
    ckt = jnp.transpose(cache_k, (0, 1, 3, 4, 5, 2)).reshape(depth, n_pool, A_WIDTH, PAGE_SIZE)
    cvh = cache_v.reshape(depth, n_pool, PAGE_SIZE * A_HEADS, A_V_DIM)

    yp = x_prompt.reshape(bp * seq, D_MODEL)
    ys = x_sample.reshape(bs * steps, D_MODEL)
    outs_p, outs_s = [], []
    for l in range(depth):
        lam_init = 0.8 - 0.6 * math.exp(-0.3 * l)
        p = _layer_weights(l, weights)
        yp, st_p = _prompt_layer(yp, p, lam_init, bp, seq)
        ys, st_s = _sample_layer(ys, p, lam_init, l, bs, steps, page_table, ckt, cvh,
                                 state_conv[l], state_rglru[l], state_gla[l])
        outs_p.append(st_p)
        outs_s.append(st_s)
    stack = lambda outs: [jnp.stack([o[i] for o in outs]) for i in range(5)]
    k_p, v_p, conv_p, h_p, s_p = stack(outs_p)
    k_s, v_s, conv_s, h_s, s_s = stack(outs_s)
    return (yp.reshape(bp, seq, D_MODEL), ys.reshape(bs, steps, D_MODEL),
            k_p, v_p, conv_p, h_p, s_p, k_s, v_s, conv_s, h_s, s_s)
```
